```python
import jax, jax.numpy as jnp
from jax import lax
import numpy as np

D_MODEL = 2048
BATCH = 1
SEQ = 16384
DEPTH = 2

GLA_HEADS = 4
GLA_V = D_MODEL // 2
GLA_DV = GLA_V // GLA_HEADS
GLA_DK = GLA_DV // 2
GLA_QK = GLA_HEADS * GLA_DK
GLA_GATE_RANK = 16
GLA_TAU = 16.0
GLA_CHUNK = 64
SGU_W = D_MODEL // 2
SGU_CHUNK = 128
SGU_GROUPS = SGU_W // 128
SB_HEAD_DIM = 128
SB_HEADS = D_MODEL // SB_HEAD_DIM
SB_W = SB_HEADS * SB_HEAD_DIM
SB_BLOCK = 128
EVEN_COL_WIDTHS = (GLA_QK, GLA_QK, GLA_V, GLA_GATE_RANK, GLA_V, SGU_W, SGU_W, SGU_W)
EVEN_COLS = sum(EVEN_COL_WIDTHS)
EVEN_MIX = GLA_V + SGU_W
ODD_COLS = 4 * SB_W
DEEPNORM_ALPHA = (2.0 * DEPTH) ** 0.25
DEEPNORM_BETA = (8.0 * DEPTH) ** -0.25
N_EVEN = (DEPTH + 1) // 2
N_ODD = DEPTH // 2
LN_EPS = 1e-5

kernel_name = "hybrid_gla_sgu_stickbreaking_deepnorm"


def layer_norm(x, g, b):
    xf = x.astype(jnp.float32)
    mu = jnp.mean(xf, axis=-1, keepdims=True)
    xc = xf - mu
    var = jnp.mean(jnp.square(xc), axis=-1, keepdims=True)
    return (xc * lax.rsqrt(var + LN_EPS) * g + b).astype(x.dtype)


def rms_norm(x, g):
    xf = x.astype(jnp.float32)
    return xf * lax.rsqrt(jnp.mean(jnp.square(xf), axis=-1, keepdims=True) + LN_EPS) * g


def gated_linear_attention(q, k, v, log_a):
    B, S, H, dk = q.shape
    dv = v.shape[-1]
    C = GLA_CHUNK
    N = S // C

    def chunks(t):
        return t.reshape(B, N, C, H, t.shape[-1]).transpose(0, 3, 1, 2, 4)

    q, k, v, log_a = chunks(q) * (dk ** -0.5), chunks(k), chunks(v), chunks(log_a)
    b = jnp.cumsum(log_a, axis=3)
    b_ref = b[:, :, :, C // 2:C // 2 + 1]
    scores = jnp.einsum('bhnik,bhnjk->bhnij', q * jnp.exp(b - b_ref), k * jnp.exp(b_ref - b))
    causal = jnp.tril(jnp.ones((C, C), dtype=bool))
    scores = jnp.where(causal, scores, 0.0)
    o = jnp.einsum('bhnij,bhnjv->bhniv', scores, v)

    b_last = b[:, :, :, -1:]
    kv = jnp.einsum('bhnck,bhncv->bhnkv', k * jnp.exp(b_last - b), v)
    decay = jnp.exp(b_last[:, :, :, 0])

    def step(state, inp):
        d, u = inp
        return d[..., None] * state + u, state

    init = jnp.zeros((B, H, dk, dv), jnp.float32)
    _, s_prev = lax.scan(step, init, (jnp.moveaxis(decay, 2, 0), jnp.moveaxis(kv, 2, 0)))
    s_prev = jnp.moveaxis(s_prev, 0, 2)
    o = o + jnp.einsum('bhnik,bhnkv->bhniv', q * jnp.exp(b), s_prev)
    return o.transpose(0, 2, 3, 1, 4).reshape(B, S, H, dv)


def spatial_gating(u, v, ln_g, ln_b, w_s, b_s):
    v = layer_norm(v, ln_g, ln_b)
    B, S, W = v.shape
    C, G = SGU_CHUNK, SGU_GROUPS
    vc = v.reshape(B, S // C, C, G, W // G)
    w = jnp.where(jnp.tril(jnp.ones((C, C), dtype=bool)), w_s, 0.0)
    s = jnp.einsum('gtp,bnpgc->bntgc', w.astype(vc.dtype), vc) + b_s.T[None, None, :, :, None]
    return u * s.reshape(B, S, W).astype(u.dtype)


def stick_breaking_attention(q, k, v):
    B, S, H, d = q.shape
    T = SB_BLOCK
    NB = S // T
    qb = q.reshape(B, NB, T, H, d).transpose(1, 0, 3, 2, 4)
    kh = k.transpose(0, 2, 1, 3)
    vh = v.transpose(0, 2, 1, 3)
    key_pos = jnp.arange(S)

    def block(args):
        q_blk, start = args
        z = jnp.einsum('bhtd,bhsd->bhts', q_blk, kh) * (d ** -0.5)
        q_pos = start + jnp.arange(T)
        strict = key_pos[None, :] < q_pos[:, None]
        log_keep = jnp.where(strict, jax.nn.log_sigmoid(-z), 0.0)
        later = lax.cumsum(log_keep, axis=3, reverse=True) - log_keep
        wts = jnp.where(strict, jnp.exp(jax.nn.log_sigmoid(z) + later), 0.0)
        return jnp.einsum('bhts,bhsd->bhtd', wts, vh)

    o = lax.map(block, (qb, jnp.arange(NB) * T))
    return o.transpose(1, 0, 3, 2, 4).reshape(B, S, H, d)


def even_layer(x, w_in, w_gate2, b_gate, gla_norm_g, sgu_ln_g, sgu_ln_b, w_s, b_s, w_out):
    B, S, _ = x.shape
    h = x @ w_in
    splits = [int(i) for i in np.cumsum(EVEN_COL_WIDTHS)[:-1]]
    qa, ka, va, ra, ga, ub, vb, gb = jnp.split(h, splits, axis=-1)
    f32 = jnp.float32
    log_a = jax.nn.log_sigmoid((ra @ w_gate2 + b_gate).astype(f32)) / GLA_TAU
    o_a = gated_linear_attention(
        qa.astype(f32).reshape(B, S, GLA_HEADS, GLA_DK),
        ka.astype(f32).reshape(B, S, GLA_HEADS, GLA_DK),
        va.astype(f32).reshape(B, S, GLA_HEADS, GLA_DV),
        log_a.reshape(B, S, GLA_HEADS, GLA_DK))
    o_a = rms_norm(o_a, gla_norm_g).reshape(B, S, GLA_V).astype(x.dtype) * jax.nn.silu(ga)
    o_b = spatial_gating(jax.nn.gelu(ub), jax.nn.gelu(vb), sgu_ln_g, sgu_ln_b, w_s, b_s) * jax.nn.silu(gb)
    return jnp.concatenate([o_a, o_b], axis=-1) @ w_out


def odd_layer(x, w_in, w_out):
    B, S, _ = x.shape
    q, k, v, g = jnp.split(x @ w_in, 4, axis=-1)
    shp = (B, S, SB_HEADS, SB_HEAD_DIM)
    o = stick_breaking_attention(q.astype(jnp.float32).reshape(shp),
                                 k.astype(jnp.float32).reshape(shp),
                                 v.astype(jnp.float32).reshape(shp))
    return (o.reshape(B, S, SB_W).astype(x.dtype) * jax.nn.silu(g)) @ w_out


def setup_inputs(seed: int = 0) -> dict:
    key = jax.random.key(seed)
    ks = jax.random.split(key, 16)
    d_scale = D_MODEL ** -0.5
    even_col_scale = np.concatenate([
        np.full(w, DEEPNORM_BETA if i in (2, 5) else 1.0, dtype=np.float32)
        for i, w in enumerate(EVEN_COL_WIDTHS)])
    odd_col_scale = np.concatenate([
        np.full(SB_W, DEEPNORM_BETA if i == 2 else 1.0, dtype=np.float32) for i in range(4)])
    x = jax.random.normal(ks[0], (BATCH, SEQ, D_MODEL), jnp.float32)
    even_w_in = jax.random.normal(ks[1], (N_EVEN, D_MODEL, EVEN_COLS), jnp.float32) * d_scale * jnp.asarray(even_col_scale)
    even_gla_w_gate2 = jax.random.normal(ks[2], (N_EVEN, GLA_GATE_RANK, GLA_QK), jnp.float32) * GLA_GATE_RANK ** -0.5
    even_gla_b_gate = 0.1 * jax.random.normal(ks[3], (N_EVEN, GLA_QK), jnp.float32)
    even_gla_norm_g = 1.0 + 0.02 * jax.random.normal(ks[4], (N_EVEN, GLA_DV), jnp.float32)
    even_sgu_ln_g = 1.0 + 0.02 * jax.random.normal(ks[5], (N_EVEN, SGU_W), jnp.float32)
    even_sgu_ln_b = 0.02 * jax.random.normal(ks[6], (N_EVEN, SGU_W), jnp.float32)
    even_sgu_w_s = jax.random.normal(ks[7], (N_EVEN, SGU_GROUPS, SGU_CHUNK, SGU_CHUNK), jnp.float32) * SGU_CHUNK ** -0.5
    even_sgu_b_s = 1.0 + 0.1 * jax.random.normal(ks[8], (N_EVEN, SGU_GROUPS, SGU_CHUNK), jnp.float32)
    even_w_out = jax.random.normal(ks[9], (N_EVEN, EVEN_MIX, D_MODEL), jnp.float32) * EVEN_MIX ** -0.5 * DEEPNORM_BETA
    odd_w_in = jax.random.normal(ks[10], (N_ODD, D_MODEL, ODD_COLS), jnp.float32) * d_scale * jnp.asarray(odd_col_scale)
    odd_w_out = jax.random.normal(ks[11], (N_ODD, SB_W, D_MODEL), jnp.float32) * SB_W ** -0.5 * DEEPNORM_BETA
    post_ln_g = 1.0 + 0.02 * jax.random.normal(ks[12], (DEPTH, D_MODEL), jnp.float32)
    post_ln_b = 0.02 * jax.random.normal(ks[13], (DEPTH, D_MODEL), jnp.float32)
    return {"x": x, "even_w_in": even_w_in, "even_gla_w_gate2": even_gla_w_gate2,
            "even_gla_b_gate": even_gla_b_gate, "even_gla_norm_g": even_gla_norm_g,
            "even_sgu_ln_g": even_sgu_ln_g, "even_sgu_ln_b": even_sgu_ln_b,
            "even_sgu_w_s": even_sgu_w_s, "even_sgu_b_s": even_sgu_b_s, "even_w_out": even_w_out,
            "odd_w_in": odd_w_in, "odd_w_out": odd_w_out,
            "post_ln_g": post_ln_g, "post_ln_b": post_ln_b}


def reference(x, even_w_in, even_gla_w_gate2, even_gla_b_gate, even_gla_norm_g,
              even_sgu_ln_g, even_sgu_ln_b, even_sgu_w_s, even_sgu_b_s, even_w_out,
              odd_w_in, odd_w_out, post_ln_g, post_ln_b):
    for layer in range(DEPTH):
        i = layer // 2
        if layer % 2 == 0:
            y = even_layer(x, even_w_in[i], even_gla_w_gate2[i], even_gla_b_gate[i], even_gla_norm_g[i],
                           even_sgu_ln_g[i], even_sgu_ln_b[i], even_sgu_w_s[i], even_sgu_b_s[i], even_w_out[i])
        else:
            y = odd_layer(x, odd_w_in[i], odd_w_out[i])
        x = layer_norm(DEEPNORM_ALPHA * x + y, post_ln_g[layer], post_ln_b[layer])
    return x
```

```python
import functools

import jax
import jax.numpy as jnp
from jax import lax
from jax.experimental import pallas as pl
from jax.experimental.pallas import tpu as pltpu

F32 = jnp.float32
BF16 = jnp.bfloat16

LANES = 128
D_MODEL = 2048
DEPTH = 2
GLA_HEADS = 4
GLA_DK = 128
GLA_DV = 256
GLA_QK = GLA_HEADS * GLA_DK
GLA_V = GLA_HEADS * GLA_DV
GLA_RANK = 16
GLA_TAU = 16.0
GLA_CHUNK = 64
SGU_W = 1024
SGU_CHUNK = 128
SGU_GROUPS = SGU_W // LANES
SB_HEADS = 16
SB_DIM = 128
ALPHA = (2.0 * DEPTH) ** 0.25
LN_EPS = 1e-5

_QA, _KA, _VA, _GA, _UB, _VB, _GB, _RA = 0, 4, 8, 16, 24, 32, 40, 48
EVEN_BLOCKS = 49

VMEM_LIMIT = 56 * 1024 * 1024


def _softplus(x):
    return jnp.maximum(x, 0.0) + jnp.log(1.0 + jnp.exp(-jnp.abs(x)))


def _silu(x):
    return x / (1.0 + jnp.exp(-x))


def _gelu_tanh(x):
    return 0.5 * x * (1.0 + jnp.tanh(0.7978845608028654 * (x + 0.044715 * (x * x * x))))


def _split_bf16(x):
    hi = x.astype(BF16)
    lo = (x - hi.astype(F32)).astype(BF16)
    return hi, lo


def _dot(a, b):
    return jnp.dot(a, b, preferred_element_type=F32)


def _dot_nt(a, b):
    return lax.dot_general(a, b, (((1,), (1,)), ((), ())), preferred_element_type=F32)


def _proj_kernel(a_ref, w_ref, o_ref):
    r = _dot(a_ref[...], w_ref[...])
    for c in range(o_ref.shape[0]):
        o_ref[c] = r[:, c * LANES:(c + 1) * LANES].astype(o_ref.dtype)


def _project(a, w, tm, tn):
    m, k = a.shape
    n = w.shape[1]
    nb = tn // LANES
    return pl.pallas_call(
        _proj_kernel,
        out_shape=jax.ShapeDtypeStruct((n // LANES, m, LANES), BF16),
        grid=(n // tn, m // tm),
        in_specs=[pl.BlockSpec((tm, k), lambda j, i: (i, 0)),
                  pl.BlockSpec((k, tn), lambda j, i: (0, j))],
        out_specs=pl.BlockSpec((nb, tm, LANES), lambda j, i: (j, i, 0)),
        compiler_params=pltpu.CompilerParams(
            dimension_semantics=("arbitrary", "arbitrary"), vmem_limit_bytes=VMEM_LIMIT),
        name="in_proj",
    )(a, w)


def _out_ln_kernel(a_ref, w_ref, x_ref, g_ref, b_ref, o_ref, ob_ref):
    y = _dot(a_ref[...], w_ref[...])
    r = ALPHA * x_ref[...] + y
    mu = jnp.mean(r, axis=-1, keepdims=True)
    rc = r - mu
    var = jnp.mean(rc * rc, axis=-1, keepdims=True)
    out = rc * lax.rsqrt(var + LN_EPS) * g_ref[...] + b_ref[...]
    o_ref[...] = out
    ob_ref[...] = out.astype(BF16)


def _out_proj_ln(a, w, x, g, b, tm):
    m, k = a.shape
    n = w.shape[1]
    return pl.pallas_call(
        _out_ln_kernel,
        out_shape=(jax.ShapeDtypeStruct((m, n), F32), jax.ShapeDtypeStruct((m, n), BF16)),
        grid=(m // tm,),
        in_specs=[pl.BlockSpec((tm, k), lambda i: (i, 0)),
                  pl.BlockSpec((k, n), lambda i: (0, 0)),
                  pl.BlockSpec((tm, n), lambda i: (i, 0)),
                  pl.BlockSpec((1, n), lambda i: (0, 0)),
                  pl.BlockSpec((1, n), lambda i: (0, 0))],
        out_specs=(pl.BlockSpec((tm, n), lambda i: (i, 0)),
                   pl.BlockSpec((tm, n), lambda i: (i, 0))),
        compiler_params=pltpu.CompilerParams(
            dimension_semantics=("arbitrary",), vmem_limit_bytes=VMEM_LIMIT),
        name="out_proj_ln",
    )(a, w, x, g, b)


def _even_mix_kernel(h_ref, wg_ref, bg_ref, gn_ref, lng_ref, lnb_ref, ws_ref, bst_ref,
                     o_ref, state_ref, wt_ref, bias_ref, *, ts):
    step = pl.program_id(0)

    @pl.when(step == 0)
    def _init():
        state_ref[...] = jnp.zeros_like(state_ref)
        r = lax.broadcasted_iota(jnp.int32, (SGU_CHUNK, SGU_CHUNK), 0)
        c = lax.broadcasted_iota(jnp.int32, (SGU_CHUNK, SGU_CHUNK), 1)
        for g in range(SGU_GROUPS):
            wt_ref[g] = jnp.where(c <= r, ws_ref[g], 0.0).astype(BF16)
            bias_ref[g] = jnp.broadcast_to(bst_ref[:, g:g + 1], (SGU_CHUNK, LANES))

    u = _dot(h_ref[_RA], wg_ref[...]) + bg_ref[...]
    log_a = -_softplus(-u) * (1.0 / GLA_TAU)
    r = lax.broadcasted_iota(jnp.int32, (ts, ts), 0)
    c = lax.broadcasted_iota(jnp.int32, (ts, ts), 1)
    same_chunk = (r // GLA_CHUNK) == (c // GLA_CHUNK)
    cum_mat = jnp.where(same_chunk & (c <= r), 1.0, 0.0).astype(BF16)
    la_hi, la_lo = _split_bf16(log_a)
    b_all = _dot(cum_mat, la_hi) + _dot(cum_mat, la_lo)

    rr = lax.broadcasted_iota(jnp.int32, (GLA_CHUNK, GLA_CHUNK), 0)
    cc = lax.broadcasted_iota(jnp.int32, (GLA_CHUNK, GLA_CHUNK), 1)
    causal = cc <= rr
    scale = GLA_DK ** -0.5
    gn = gn_ref[...]

    for ck in range(ts // GLA_CHUNK):
        rows = slice(ck * GLA_CHUNK, (ck + 1) * GLA_CHUNK)
        bc = b_all[rows]
        b_mid = bc[GLA_CHUNK // 2:GLA_CHUNK // 2 + 1]
        b_last = bc[GLA_CHUNK - 1:GLA_CHUNK]
        e_q = jnp.exp(bc - b_mid) * scale
        e_k = jnp.exp(b_mid - bc)
        e_kv = jnp.exp(b_last - bc)
        e_b = jnp.exp(bc) * scale
        decay = jnp.exp(b_last)
        for hd in range(GLA_HEADS):
            sl = slice(hd * GLA_DK, (hd + 1) * GLA_DK)
            qh = h_ref[_QA + hd, rows, :].astype(F32)
            kh = h_ref[_KA + hd, rows, :].astype(F32)
            vh = jnp.concatenate([h_ref[_VA + 2 * hd, rows, :], h_ref[_VA + 2 * hd + 1, rows, :]], axis=1)
            sc = _dot_nt((qh * e_q[:, sl]).astype(BF16), (kh * e_k[:, sl]).astype(BF16))
            sc = jnp.where(causal, sc, 0.0).astype(BF16)
            o = _dot(sc, vh)
            st = state_ref[hd]
            o = o + _dot_nt((qh * e_b[:, sl]).astype(BF16), st.astype(BF16))
            kk = (kh * e_kv[:, sl]).astype(BF16)
            kv_t = lax.dot_general(vh, kk, (((0,), (0,)), ((), ())), preferred_element_type=F32)
            state_ref[hd] = st * decay[:, sl] + kv_t
            ms = jnp.mean(o * o, axis=-1, keepdims=True)
            on = o * lax.rsqrt(ms + LN_EPS) * gn
            ga = jnp.concatenate([h_ref[_GA + 2 * hd, rows, :], h_ref[_GA + 2 * hd + 1, rows, :]],
                                 axis=1).astype(F32)
            o_ref[rows, hd * GLA_DV:(hd + 1) * GLA_DV] = (on * _silu(ga)).astype(o_ref.dtype)

    vg = [_gelu_tanh(h_ref[_VB + g].astype(F32)) for g in range(SGU_GROUPS)]
    tot = vg[0]
    for g in range(1, SGU_GROUPS):
        tot = tot + vg[g]
    mu = jnp.sum(tot, axis=-1, keepdims=True) * (1.0 / SGU_W)
    vc = [v - mu for v in vg]
    sq = vc[0] * vc[0]
    for g in range(1, SGU_GROUPS):
        sq = sq + vc[g] * vc[g]
    rstd = lax.rsqrt(jnp.sum(sq, axis=-1, keepdims=True) * (1.0 / SGU_W) + LN_EPS)
    for g in range(SGU_GROUPS):
        cols = slice(g * LANES, (g + 1) * LANES)
        vn = (vc[g] * rstd * lng_ref[:, cols] + lnb_ref[:, cols]).astype(BF16)
        ug = _gelu_tanh(h_ref[_UB + g].astype(F32))
        gate = _silu(h_ref[_GB + g].astype(F32))
        for ck in range(ts // SGU_CHUNK):
            rows = slice(ck * SGU_CHUNK, (ck + 1) * SGU_CHUNK)
            s = _dot(wt_ref[g], vn[rows]) + bias_ref[g]
            o_ref[rows, GLA_V + g * LANES:GLA_V + (g + 1) * LANES] = (
                ug[rows] * s * gate[rows]).astype(o_ref.dtype)


def _even_mix(h3, wg, bg, gn, lng, lnb, ws, bst, ts):
    nblk, s, _ = h3.shape
    return pl.pallas_call(
        functools.partial(_even_mix_kernel, ts=ts),
        out_shape=jax.ShapeDtypeStruct((s, GLA_V + SGU_W), BF16),
        grid=(s // ts,),
        in_specs=[pl.BlockSpec((nblk, ts, LANES), lambda i: (0, i, 0)),
                  pl.BlockSpec(wg.shape, lambda i: (0, 0)),
                  pl.BlockSpec(bg.shape, lambda i: (0, 0)),
                  pl.BlockSpec(gn.shape, lambda i: (0, 0)),
                  pl.BlockSpec(lng.shape, lambda i: (0, 0)),
                  pl.BlockSpec(lnb.shape, lambda i: (0, 0)),
                  pl.BlockSpec(ws.shape, lambda i: (0, 0, 0)),
                  pl.BlockSpec(bst.shape, lambda i: (0, 0))],
        out_specs=pl.BlockSpec((ts, GLA_V + SGU_W), lambda i: (i, 0)),
        scratch_shapes=[pltpu.VMEM((GLA_HEADS, GLA_DV, GLA_DK), F32),
                        pltpu.VMEM((SGU_GROUPS, SGU_CHUNK, SGU_CHUNK), BF16),
                        pltpu.VMEM((SGU_GROUPS, SGU_CHUNK, LANES), F32)],
        compiler_params=pltpu.CompilerParams(
            dimension_semantics=("arbitrary",), vmem_limit_bytes=VMEM_LIMIT),
        name="even_mix",
    )(h3, wg, bg, gn, lng, lnb, ws, bst)


def _sb_kernel(q_ref, k_ref, v_ref, g_ref, o_ref, acc_ref, carry_ref, *, blk):
    i = pl.program_id(1)
    scale = SB_DIM ** -0.5
    q = q_ref[...]
    r = lax.broadcasted_iota(jnp.int32, (blk, blk), 0)
    c = lax.broadcasted_iota(jnp.int32, (blk, blk), 1)
    strict = c < r
    rs = lax.broadcasted_iota(jnp.int32, (blk, blk + LANES), 0)
    cs = lax.broadcasted_iota(jnp.int32, (blk, blk + LANES), 1)
    suffix = jnp.where((rs > cs) | (cs >= blk), 1.0, 0.0).astype(BF16)

    def block(j, masked):
        start = pl.multiple_of(j * blk, blk)
        kj = k_ref[pl.ds(start, blk), :]
        vj = v_ref[pl.ds(start, blk), :]
        z = _dot_nt(q, kj) * scale
        sp = _softplus(z)
        log_keep = -sp
        if masked:
            log_keep = jnp.where(strict, log_keep, 0.0)
        hi, lo = _split_bf16(log_keep)
        cum = _dot(hi, suffix) + _dot(lo, suffix)
        later = cum[:, :blk]
        if not masked:
            later = later + pltpu.repeat(carry_ref[...], blk // LANES, axis=1)
        w = jnp.exp(z - sp + later)
        if masked:
            w = jnp.where(strict, w, 0.0)
        pv = _dot(w.astype(BF16), vj)
        if masked:
            acc_ref[...] = pv
            carry_ref[...] = cum[:, blk:]
        else:
            acc_ref[...] += pv
            carry_ref[...] += cum[:, blk:]

    block(i, True)

    def body(t, _):
        block(i - 1 - t, False)
        return 0

    lax.fori_loop(0, i, body, 0)
    o_ref[...] = (acc_ref[...] * _silu(g_ref[...].astype(F32))).astype(o_ref.dtype)


def _stick_breaking(h3, blk):
    _, s, _ = h3.shape
    return pl.pallas_call(
        functools.partial(_sb_kernel, blk=blk),
        out_shape=jax.ShapeDtypeStruct((s, SB_HEADS * SB_DIM), BF16),
        grid=(SB_HEADS, s // blk),
        in_specs=[pl.BlockSpec((None, blk, SB_DIM), lambda h, i: (h, i, 0)),
                  pl.BlockSpec((None, s, SB_DIM), lambda h, i: (SB_HEADS + h, 0, 0)),
                  pl.BlockSpec((None, s, SB_DIM), lambda h, i: (2 * SB_HEADS + h, 0, 0)),
                  pl.BlockSpec((None, blk, SB_DIM), lambda h, i: (3 * SB_HEADS + h, i, 0))],
        out_specs=pl.BlockSpec((blk, SB_DIM), lambda h, i: (i, h)),
        scratch_shapes=[pltpu.VMEM((blk, SB_DIM), F32), pltpu.VMEM((blk, LANES), F32)],
        compiler_params=pltpu.CompilerParams(
            dimension_semantics=("arbitrary", "arbitrary"), vmem_limit_bytes=VMEM_LIMIT),
        name="stick_breaking",
    )(h3, h3, h3, h3)


def _pick(n, candidates):
    for c in candidates:
        if n % c == 0:
            return c
    raise ValueError(f"no tile of {candidates} divides {n}")


def kernel(x, even_w_in, even_gla_w_gate2, even_gla_b_gate, even_gla_norm_g, even_sgu_ln_g, even_sgu_ln_b,
           even_sgu_w_s, even_sgu_b_s, even_w_out, odd_w_in, odd_w_out, post_ln_g, post_ln_b):
    bsz, seq, d = x.shape
    assert bsz == 1 and d == D_MODEL
    xs = x[0]
    tm = _pick(seq, (1024, 512, 256))
    tl = _pick(seq, (256,))

    w = even_w_in[0]
    o_q, o_k, o_v, o_r = 0, GLA_QK, 2 * GLA_QK, 2 * GLA_QK + GLA_V
    o_g = o_r + GLA_RANK
    w_even = jnp.concatenate(
        [w[:, o_q:o_r], w[:, o_g:], w[:, o_r:o_g], jnp.zeros((d, LANES - GLA_RANK), w.dtype)],
        axis=1).astype(BF16)
    wg = jnp.concatenate([even_gla_w_gate2[0], jnp.zeros((LANES - GLA_RANK, GLA_QK), F32)], axis=0).astype(BF16)
    h3 = _project(xs.astype(BF16), w_even, tm, 7 * LANES)
    mix = _even_mix(h3, wg, even_gla_b_gate[0][None, :], even_gla_norm_g[0][None, :],
                    even_sgu_ln_g[0][None, :], even_sgu_ln_b[0][None, :],
                    even_sgu_w_s[0], even_sgu_b_s[0].T, tl)
    x1, x1b = _out_proj_ln(mix, even_w_out[0].astype(BF16), xs, post_ln_g[0][None, :], post_ln_b[0][None, :], tl)

    h3 = _project(x1b, odd_w_in[0].astype(BF16), tm, 1024)
    att = _stick_breaking(h3, 256)
    x2, _ = _out_proj_ln(att, odd_w_out[0].astype(BF16), x1, post_ln_g[1][None, :], post_ln_b[1][None, :], tl)
    return x2[None]
```

```python
import functools

import jax
import jax.numpy as jnp
from jax import lax
from jax.experimental import pallas as pl
from jax.experimental.pallas import tpu as pltpu

F32 = jnp.float32
BF16 = jnp.bfloat16

LANES = 128
D_MODEL = 2048
DEPTH = 2
GLA_HEADS = 4
GLA_DK = 128
GLA_DV = 256
GLA_QK = GLA_HEADS * GLA_DK
GLA_V = GLA_HEADS * GLA_DV
GLA_RANK = 16
GLA_TAU = 16.0
GLA_CHUNK = 64
SGU_W = 1024
SGU_CHUNK = 128
SGU_GROUPS = SGU_W // LANES
SB_HEADS = 16
SB_DIM = 128
ALPHA = (2.0 * DEPTH) ** 0.25
LN_EPS = 1e-5

_QA, _KA, _VA, _GA, _UB, _VB, _GB, _RA = 0, 4, 8, 16, 24, 32, 40, 48
EVEN_BLOCKS = 49

VMEM_LIMIT = 56 * 1024 * 1024
F32_EXP_UNDERFLOW = -104.0


def _softplus(x):
    return jnp.maximum(x, 0.0) + jnp.log(1.0 + jnp.exp(-jnp.abs(x)))


def _silu(x):
    return x / (1.0 + jnp.exp(-x))


def _gelu_tanh(x):
    return 0.5 * x * (1.0 + jnp.tanh(0.7978845608028654 * (x + 0.044715 * (x * x * x))))


def _split_bf16(x):
    hi = x.astype(BF16)
    lo = (x - hi.astype(F32)).astype(BF16)
    return hi, lo


def _dot(a, b):
    return jnp.dot(a, b, preferred_element_type=F32)


def _dot_nt(a, b):
    return lax.dot_general(a, b, (((1,), (1,)), ((), ())), preferred_element_type=F32)


def _proj_kernel(a_ref, w_ref, o_ref):
    r = _dot(a_ref[...], w_ref[...])
    for c in range(o_ref.shape[0]):
        o_ref[c] = r[:, c * LANES:(c + 1) * LANES].astype(o_ref.dtype)


def _project(a, w, tm, tn):
    m, k = a.shape
    n = w.shape[1]
    nb = tn // LANES
    return pl.pallas_call(
        _proj_kernel,
        out_shape=jax.ShapeDtypeStruct((n // LANES, m, LANES), BF16),
        grid=(n // tn, m // tm),
        in_specs=[pl.BlockSpec((tm, k), lambda j, i: (i, 0)),
                  pl.BlockSpec((k, tn), lambda j, i: (0, j))],
        out_specs=pl.BlockSpec((nb, tm, LANES), lambda j, i: (j, i, 0)),
        compiler_params=pltpu.CompilerParams(
            dimension_semantics=("arbitrary", "arbitrary"), vmem_limit_bytes=VMEM_LIMIT),
        name="in_proj",
    )(a, w)


def _out_ln_kernel(a_ref, w_ref, x_ref, g_ref, b_ref, o_ref, ob_ref):
    y = _dot(a_ref[...], w_ref[...])
    r = ALPHA * x_ref[...] + y
    mu = jnp.mean(r, axis=-1, keepdims=True)
    rc = r - mu
    var = jnp.mean(rc * rc, axis=-1, keepdims=True)
    out = rc * lax.rsqrt(var + LN_EPS) * g_ref[...] + b_ref[...]
    o_ref[...] = out
    ob_ref[...] = out.astype(BF16)


def _out_proj_ln(a, w, x, g, b, tm):
    m, k = a.shape
    n = w.shape[1]
    return pl.pallas_call(
        _out_ln_kernel,
        out_shape=(jax.ShapeDtypeStruct((m, n), F32), jax.ShapeDtypeStruct((m, n), BF16)),
        grid=(m // tm,),
        in_specs=[pl.BlockSpec((tm, k), lambda i: (i, 0)),
                  pl.BlockSpec((k, n), lambda i: (0, 0)),
                  pl.BlockSpec((tm, n), lambda i: (i, 0)),
                  pl.BlockSpec((1, n), lambda i: (0, 0)),
                  pl.BlockSpec((1, n), lambda i: (0, 0))],
        out_specs=(pl.BlockSpec((tm, n), lambda i: (i, 0)),
                   pl.BlockSpec((tm, n), lambda i: (i, 0))),
        compiler_params=pltpu.CompilerParams(
            dimension_semantics=("arbitrary",), vmem_limit_bytes=VMEM_LIMIT),
        name="out_proj_ln",
    )(a, w, x, g, b)


def _even_mix_kernel(h_ref, wg_ref, bg_ref, gn_ref, lng_ref, lnb_ref, ws_ref, bst_ref,
                     o_ref, state_ref, wt_ref, bias_ref, *, ts):
    step = pl.program_id(0)

    @pl.when(step == 0)
    def _init():
        state_ref[...] = jnp.zeros_like(state_ref)
        r = lax.broadcasted_iota(jnp.int32, (SGU_CHUNK, SGU_CHUNK), 0)
        c = lax.broadcasted_iota(jnp.int32, (SGU_CHUNK, SGU_CHUNK), 1)
        for g in range(SGU_GROUPS):
            wt_ref[g] = jnp.where(c <= r, ws_ref[g], 0.0).astype(BF16)
            bias_ref[g] = jnp.broadcast_to(bst_ref[:, g:g + 1], (SGU_CHUNK, LANES))

    u = _dot(h_ref[_RA], wg_ref[...]) + bg_ref[...]
    log_a = -_softplus(-u) * (1.0 / GLA_TAU)
    r = lax.broadcasted_iota(jnp.int32, (ts, ts), 0)
    c = lax.broadcasted_iota(jnp.int32, (ts, ts), 1)
    same_chunk = (r // GLA_CHUNK) == (c // GLA_CHUNK)
    cum_mat = jnp.where(same_chunk & (c <= r), 1.0, 0.0).astype(BF16)
    la_hi, la_lo = _split_bf16(log_a)
    b_all = _dot(cum_mat, la_hi) + _dot(cum_mat, la_lo)

    rr = lax.broadcasted_iota(jnp.int32, (GLA_CHUNK, GLA_CHUNK), 0)
    cc = lax.broadcasted_iota(jnp.int32, (GLA_CHUNK, GLA_CHUNK), 1)
    causal = cc <= rr
    scale = GLA_DK ** -0.5
    gn = gn_ref[...]

    for ck in range(ts // GLA_CHUNK):
        rows = slice(ck * GLA_CHUNK, (ck + 1) * GLA_CHUNK)
        bc = b_all[rows]
        b_mid = bc[GLA_CHUNK // 2:GLA_CHUNK // 2 + 1]
        b_last = bc[GLA_CHUNK - 1:GLA_CHUNK]
        e_q = jnp.exp(bc - b_mid) * scale
        e_k = jnp.exp(b_mid - bc)
        e_kv = jnp.exp(b_last - bc)
        e_b = jnp.exp(bc) * scale
        decay = jnp.exp(b_last)
        for hd in range(GLA_HEADS):
            sl = slice(hd * GLA_DK, (hd + 1) * GLA_DK)
            qh = h_ref[_QA + hd, rows, :].astype(F32)
            kh = h_ref[_KA + hd, rows, :].astype(F32)
            vh = jnp.concatenate([h_ref[_VA + 2 * hd, rows, :], h_ref[_VA + 2 * hd + 1, rows, :]], axis=1)
            sc = _dot_nt((qh * e_q[:, sl]).astype(BF16), (kh * e_k[:, sl]).astype(BF16))
            sc = jnp.where(causal, sc, 0.0).astype(BF16)
            o = _dot(sc, vh)
            st = state_ref[hd]
            o = o + _dot_nt((qh * e_b[:, sl]).astype(BF16), st.astype(BF16))
            kk = (kh * e_kv[:, sl]).astype(BF16)
            kv_t = lax.dot_general(vh, kk, (((0,), (0,)), ((), ())), preferred_element_type=F32)
            state_ref[hd] = st * decay[:, sl] + kv_t
            ms = jnp.mean(o * o, axis=-1, keepdims=True)
            on = o * lax.rsqrt(ms + LN_EPS) * gn
            ga = jnp.concatenate([h_ref[_GA + 2 * hd, rows, :], h_ref[_GA + 2 * hd + 1, rows, :]],
                                 axis=1).astype(F32)
            o_ref[rows, hd * GLA_DV:(hd + 1) * GLA_DV] = (on * _silu(ga)).astype(o_ref.dtype)

    vg = [_gelu_tanh(h_ref[_VB + g].astype(F32)) for g in range(SGU_GROUPS)]
    tot = vg[0]
    for g in range(1, SGU_GROUPS):
        tot = tot + vg[g]
    mu = jnp.sum(tot, axis=-1, keepdims=True) * (1.0 / SGU_W)
    vc = [v - mu for v in vg]
    sq = vc[0] * vc[0]
    for g in range(1, SGU_GROUPS):
        sq = sq + vc[g] * vc[g]
    rstd = lax.rsqrt(jnp.sum(sq, axis=-1, keepdims=True) * (1.0 / SGU_W) + LN_EPS)
    for g in range(SGU_GROUPS):
        cols = slice(g * LANES, (g + 1) * LANES)
        vn = (vc[g] * rstd * lng_ref[:, cols] + lnb_ref[:, cols]).astype(BF16)
        ug = _gelu_tanh(h_ref[_UB + g].astype(F32))
        gate = _silu(h_ref[_GB + g].astype(F32))
        for ck in range(ts // SGU_CHUNK):
            rows = slice(ck * SGU_CHUNK, (ck + 1) * SGU_CHUNK)
            s = _dot(wt_ref[g], vn[rows]) + bias_ref[g]
            o_ref[rows, GLA_V + g * LANES:GLA_V + (g + 1) * LANES] = (
                ug[rows] * s * gate[rows]).astype(o_ref.dtype)


def _even_mix(h3, wg, bg, gn, lng, lnb, ws, bst, ts):
    nblk, s, _ = h3.shape
    return pl.pallas_call(
        functools.partial(_even_mix_kernel, ts=ts),
        out_shape=jax.ShapeDtypeStruct((s, GLA_V + SGU_W), BF16),
        grid=(s // ts,),
        in_specs=[pl.BlockSpec((nblk, ts, LANES), lambda i: (0, i, 0)),
                  pl.BlockSpec(wg.shape, lambda i: (0, 0)),
                  pl.BlockSpec(bg.shape, lambda i: (0, 0)),
                  pl.BlockSpec(gn.shape, lambda i: (0, 0)),
                  pl.BlockSpec(lng.shape, lambda i: (0, 0)),
                  pl.BlockSpec(lnb.shape, lambda i: (0, 0)),
                  pl.BlockSpec(ws.shape, lambda i: (0, 0, 0)),
                  pl.BlockSpec(bst.shape, lambda i: (0, 0))],
        out_specs=pl.BlockSpec((ts, GLA_V + SGU_W), lambda i: (i, 0)),
        scratch_shapes=[pltpu.VMEM((GLA_HEADS, GLA_DV, GLA_DK), F32),
                        pltpu.VMEM((SGU_GROUPS, SGU_CHUNK, SGU_CHUNK), BF16),
                        pltpu.VMEM((SGU_GROUPS, SGU_CHUNK, LANES), F32)],
        compiler_params=pltpu.CompilerParams(
            dimension_semantics=("arbitrary",), vmem_limit_bytes=VMEM_LIMIT),
        name="even_mix",
    )(h3, wg, bg, gn, lng, lnb, ws, bst)


def _sb_kernel(q_ref, k_ref, v_ref, g_ref, o_ref, acc_ref, carry_ref, *, blk):
    i = pl.program_id(1)
    scale = SB_DIM ** -0.5
    q = q_ref[...]
    r = lax.broadcasted_iota(jnp.int32, (blk, blk), 0)
    c = lax.broadcasted_iota(jnp.int32, (blk, blk), 1)
    strict = c < r
    rs = lax.broadcasted_iota(jnp.int32, (blk, blk + LANES), 0)
    cs = lax.broadcasted_iota(jnp.int32, (blk, blk + LANES), 1)
    suffix = jnp.where((rs > cs) | (cs >= blk), 1.0, 0.0).astype(BF16)

    def block(j, masked):
        start = pl.multiple_of(j * blk, blk)
        kj = k_ref[pl.ds(start, blk), :]
        vj = v_ref[pl.ds(start, blk), :]
        z = _dot_nt(q, kj) * scale
        sp = _softplus(z)
        log_keep = -sp
        if masked:
            log_keep = jnp.where(strict, log_keep, 0.0)
        hi, lo = _split_bf16(log_keep)
        cum = _dot(hi, suffix) + _dot(lo, suffix)
        later = cum[:, :blk]
        if not masked:
            later = later + pltpu.repeat(carry_ref[...], blk // LANES, axis=1)
        w = jnp.exp(z - sp + later)
        if masked:
            w = jnp.where(strict, w, 0.0)
        pv = _dot(w.astype(BF16), vj)
        if masked:
            acc_ref[...] = pv
            carry_ref[...] = cum[:, blk:]
        else:
            acc_ref[...] += pv
            carry_ref[...] += cum[:, blk:]

    block(i, True)

    def more(state):
        j, top = state
        return jnp.logical_and(j >= 0, top > F32_EXP_UNDERFLOW)

    def body(state):
        j, _ = state
        block(j, False)
        return j - 1, jnp.max(carry_ref[...])

    lax.while_loop(more, body, (i - 1, jnp.max(carry_ref[...])))
    o_ref[...] = (acc_ref[...] * _silu(g_ref[...].astype(F32))).astype(o_ref.dtype)


def _stick_breaking(h3, blk):
    _, s, _ = h3.shape
    return pl.pallas_call(
        functools.partial(_sb_kernel, blk=blk),
        out_shape=jax.ShapeDtypeStruct((s, SB_HEADS * SB_DIM), BF16),
        grid=(SB_HEADS, s // blk),
        in_specs=[pl.BlockSpec((None, blk, SB_DIM), lambda h, i: (h, i, 0)),
                  pl.BlockSpec((None, s, SB_DIM), lambda h, i: (SB_HEADS + h, 0, 0)),
                  pl.BlockSpec((None, s, SB_DIM), lambda h, i: (2 * SB_HEADS + h, 0, 0)),
                  pl.BlockSpec((None, blk, SB_DIM), lambda h, i: (3 * SB_HEADS + h, i, 0))],
        out_specs=pl.BlockSpec((blk, SB_DIM), lambda h, i: (i, h)),
        scratch_shapes=[pltpu.VMEM((blk, SB_DIM), F32), pltpu.VMEM((blk, LANES), F32)],
        compiler_params=pltpu.CompilerParams(
            dimension_semantics=("arbitrary", "arbitrary"), vmem_limit_bytes=VMEM_LIMIT),
        name="stick_breaking",
    )(h3, h3, h3, h3)


def _pick(n, candidates):
    for c in candidates:
        if n % c == 0:
            return c
    raise ValueError(f"no tile of {candidates} divides {n}")


def kernel(x, even_w_in, even_gla_w_gate2, even_gla_b_gate, even_gla_norm_g, even_sgu_ln_g, even_sgu_ln_b,
           even_sgu_w_s, even_sgu_b_s, even_w_out, odd_w_in, odd_w_out, post_ln_g, post_ln_b):
    bsz, seq, d = x.shape
    assert bsz == 1 and d == D_MODEL
    xs = x[0]
    tm = _pick(seq, (1024, 512, 256))
    tl = _pick(seq, (256,))

    w = even_w_in[0]
    o_q, o_k, o_v, o_r = 0, GLA_QK, 2 * GLA_QK, 2 * GLA_QK + GLA_V
    o_g = o_r + GLA_RANK
    w_even = jnp.concatenate(
        [w[:, o_q:o_r], w[:, o_g:], w[:, o_r:o_g], jnp.zeros((d, LANES - GLA_RANK), w.dtype)],
        axis=1).astype(BF16)
    wg = jnp.concatenate([even_gla_w_gate2[0], jnp.zeros((LANES - GLA_RANK, GLA_QK), F32)], axis=0).astype(BF16)
    h3 = _project(xs.astype(BF16), w_even, tm, 7 * LANES)
    mix = _even_mix(h3, wg, even_gla_b_gate[0][None, :], even_gla_norm_g[0][None, :],
                    even_sgu_ln_g[0][None, :], even_sgu_ln_b[0][None, :],
                    even_sgu_w_s[0], even_sgu_b_s[0].T, tl)
    x1, x1b = _out_proj_ln(mix, even_w_out[0].astype(BF16), xs, post_ln_g[0][None, :], post_ln_b[0][None, :], tl)

    h3 = _project(x1b, odd_w_in[0].astype(BF16), tm, 1024)
    att = _stick_breaking(h3, 256)
    x2, _ = _out_proj_ln(att, odd_w_out[0].astype(BF16), x1, post_ln_g[1][None, :], post_ln_b[1][None, :], tl)
    return x2[None]
```

```python
import functools

import jax
import jax.numpy as jnp
from jax import lax
from jax.experimental import pallas as pl
from jax.experimental.pallas import tpu as pltpu

F32 = jnp.float32
BF16 = jnp.bfloat16

LANES = 128
D_MODEL = 2048
DEPTH = 2
GLA_HEADS = 4
GLA_DK = 128
GLA_DV = 256
GLA_QK = GLA_HEADS * GLA_DK
GLA_V = GLA_HEADS * GLA_DV
GLA_RANK = 16
GLA_TAU = 16.0
GLA_CHUNK = 64
SGU_W = 1024
SGU_CHUNK = 128
SGU_GROUPS = SGU_W // LANES
SB_HEADS = 16
SB_DIM = 128
ALPHA = (2.0 * DEPTH) ** 0.25
LN_EPS = 1e-5

_QA, _KA, _VA, _GA, _UB, _VB, _GB, _RA = 0, 4, 8, 16, 24, 32, 40, 48
EVEN_BLOCKS = 49

VMEM_LIMIT = 56 * 1024 * 1024
LOG2E = 1.4426950408889634
F32_EXP2_UNDERFLOW = 151.0


def _softplus(x):
    return jnp.maximum(x, 0.0) + jnp.log(1.0 + jnp.exp(-jnp.abs(x)))


def _silu(x):
    return x / (1.0 + jnp.exp(-x))


def _gelu_tanh(x):
    return 0.5 * x * (1.0 + jnp.tanh(0.7978845608028654 * (x + 0.044715 * (x * x * x))))


def _split_bf16(x):
    hi = x.astype(BF16)
    lo = (x - hi.astype(F32)).astype(BF16)
    return hi, lo


def _dot(a, b):
    return jnp.dot(a, b, preferred_element_type=F32)


def _dot_nt(a, b):
    return lax.dot_general(a, b, (((1,), (1,)), ((), ())), preferred_element_type=F32)


def _proj_kernel(a_ref, w_ref, cs_ref, o_ref):
    r = _dot(a_ref[...], w_ref[...]) * cs_ref[...]
    for c in range(o_ref.shape[0]):
        o_ref[c] = r[:, c * LANES:(c + 1) * LANES].astype(o_ref.dtype)


def _project(a, w, col_scale, tm, tn):
    m, k = a.shape
    n = w.shape[1]
    nb = tn // LANES
    return pl.pallas_call(
        _proj_kernel,
        out_shape=jax.ShapeDtypeStruct((n // LANES, m, LANES), BF16),
        grid=(n // tn, m // tm),
        in_specs=[pl.BlockSpec((tm, k), lambda j, i: (i, 0)),
                  pl.BlockSpec((k, tn), lambda j, i: (0, j)),
                  pl.BlockSpec((1, tn), lambda j, i: (0, j))],
        out_specs=pl.BlockSpec((nb, tm, LANES), lambda j, i: (j, i, 0)),
        compiler_params=pltpu.CompilerParams(
            dimension_semantics=("arbitrary", "arbitrary"), vmem_limit_bytes=VMEM_LIMIT),
        name="in_proj",
    )(a, w, col_scale)


def _out_ln_kernel(a_ref, w_ref, x_ref, g_ref, b_ref, o_ref, ob_ref):
    y = _dot(a_ref[...], w_ref[...])
    r = ALPHA * x_ref[...] + y
    mu = jnp.mean(r, axis=-1, keepdims=True)
    rc = r - mu
    var = jnp.mean(rc * rc, axis=-1, keepdims=True)
    out = rc * lax.rsqrt(var + LN_EPS) * g_ref[...] + b_ref[...]
    o_ref[...] = out
    ob_ref[...] = out.astype(BF16)


def _out_proj_ln(a, w, x, g, b, tm):
    m, k = a.shape
    n = w.shape[1]
    return pl.pallas_call(
        _out_ln_kernel,
        out_shape=(jax.ShapeDtypeStruct((m, n), F32), jax.ShapeDtypeStruct((m, n), BF16)),
        grid=(m // tm,),
        in_specs=[pl.BlockSpec((tm, k), lambda i: (i, 0)),
                  pl.BlockSpec((k, n), lambda i: (0, 0)),
                  pl.BlockSpec((tm, n), lambda i: (i, 0)),
                  pl.BlockSpec((1, n), lambda i: (0, 0)),
                  pl.BlockSpec((1, n), lambda i: (0, 0))],
        out_specs=(pl.BlockSpec((tm, n), lambda i: (i, 0)),
                   pl.BlockSpec((tm, n), lambda i: (i, 0))),
        compiler_params=pltpu.CompilerParams(
            dimension_semantics=("arbitrary",), vmem_limit_bytes=VMEM_LIMIT),
        name="out_proj_ln",
    )(a, w, x, g, b)


def _even_mix_kernel(h_ref, wg_ref, bg_ref, gn_ref, lng_ref, lnb_ref, ws_ref, bst_ref,
                     o_ref, state_ref, wt_ref, bias_ref, *, ts):
    step = pl.program_id(0)

    @pl.when(step == 0)
    def _init():
        state_ref[...] = jnp.zeros_like(state_ref)
        r = lax.broadcasted_iota(jnp.int32, (SGU_CHUNK, SGU_CHUNK), 0)
        c = lax.broadcasted_iota(jnp.int32, (SGU_CHUNK, SGU_CHUNK), 1)
        for g in range(SGU_GROUPS):
            wt_ref[g] = jnp.where(c <= r, ws_ref[g], 0.0).astype(BF16)
            bias_ref[g] = jnp.broadcast_to(bst_ref[:, g:g + 1], (SGU_CHUNK, LANES))

    u = _dot(h_ref[_RA], wg_ref[...]) + bg_ref[...]
    log_a = -_softplus(-u) * (1.0 / GLA_TAU)
    r = lax.broadcasted_iota(jnp.int32, (ts, ts), 0)
    c = lax.broadcasted_iota(jnp.int32, (ts, ts), 1)
    same_chunk = (r // GLA_CHUNK) == (c // GLA_CHUNK)
    cum_mat = jnp.where(same_chunk & (c <= r), 1.0, 0.0).astype(BF16)
    la_hi, la_lo = _split_bf16(log_a)
    b_all = _dot(cum_mat, la_hi) + _dot(cum_mat, la_lo)

    rr = lax.broadcasted_iota(jnp.int32, (GLA_CHUNK, GLA_CHUNK), 0)
    cc = lax.broadcasted_iota(jnp.int32, (GLA_CHUNK, GLA_CHUNK), 1)
    causal = cc <= rr
    scale = GLA_DK ** -0.5
    gn = gn_ref[...]

    for ck in range(ts // GLA_CHUNK):
        rows = slice(ck * GLA_CHUNK, (ck + 1) * GLA_CHUNK)
        bc = b_all[rows]
        b_mid = bc[GLA_CHUNK // 2:GLA_CHUNK // 2 + 1]
        b_last = bc[GLA_CHUNK - 1:GLA_CHUNK]
        e_q = jnp.exp(bc - b_mid) * scale
        e_k = jnp.exp(b_mid - bc)
        e_kv = jnp.exp(b_last - bc)
        e_b = jnp.exp(bc) * scale
        decay = jnp.exp(b_last)
        for hd in range(GLA_HEADS):
            sl = slice(hd * GLA_DK, (hd + 1) * GLA_DK)
            qh = h_ref[_QA + hd, rows, :].astype(F32)
            kh = h_ref[_KA + hd, rows, :].astype(F32)
            vh = jnp.concatenate([h_ref[_VA + 2 * hd, rows, :], h_ref[_VA + 2 * hd + 1, rows, :]], axis=1)
            sc = _dot_nt((qh * e_q[:, sl]).astype(BF16), (kh * e_k[:, sl]).astype(BF16))
            sc = jnp.where(causal, sc, 0.0).astype(BF16)
            o = _dot(sc, vh)
            st = state_ref[hd]
            o = o + _dot_nt((qh * e_b[:, sl]).astype(BF16), st.astype(BF16))
            kk = (kh * e_kv[:, sl]).astype(BF16)
            kv_t = lax.dot_general(vh, kk, (((0,), (0,)), ((), ())), preferred_element_type=F32)
            state_ref[hd] = st * decay[:, sl] + kv_t
            ms = jnp.mean(o * o, axis=-1, keepdims=True)
            on = o * lax.rsqrt(ms + LN_EPS) * gn
            ga = jnp.concatenate([h_ref[_GA + 2 * hd, rows, :], h_ref[_GA + 2 * hd + 1, rows, :]],
                                 axis=1).astype(F32)
            o_ref[rows, hd * GLA_DV:(hd + 1) * GLA_DV] = (on * _silu(ga)).astype(o_ref.dtype)

    vg = [_gelu_tanh(h_ref[_VB + g].astype(F32)) for g in range(SGU_GROUPS)]
    tot = vg[0]
    for g in range(1, SGU_GROUPS):
        tot = tot + vg[g]
    mu = jnp.sum(tot, axis=-1, keepdims=True) * (1.0 / SGU_W)
    vc = [v - mu for v in vg]
    sq = vc[0] * vc[0]
    for g in range(1, SGU_GROUPS):
        sq = sq + vc[g] * vc[g]
    rstd = lax.rsqrt(jnp.sum(sq, axis=-1, keepdims=True) * (1.0 / SGU_W) + LN_EPS)
    for g in range(SGU_GROUPS):
        cols = slice(g * LANES, (g + 1) * LANES)
        vn = (vc[g] * rstd * lng_ref[:, cols] + lnb_ref[:, cols]).astype(BF16)
        ug = _gelu_tanh(h_ref[_UB + g].astype(F32))
        gate = _silu(h_ref[_GB + g].astype(F32))
        for ck in range(ts // SGU_CHUNK):
            rows = slice(ck * SGU_CHUNK, (ck + 1) * SGU_CHUNK)
            s = _dot(wt_ref[g], vn[rows]) + bias_ref[g]
            o_ref[rows, GLA_V + g * LANES:GLA_V + (g + 1) * LANES] = (
                ug[rows] * s * gate[rows]).astype(o_ref.dtype)


def _even_mix(h3, wg, bg, gn, lng, lnb, ws, bst, ts):
    nblk, s, _ = h3.shape
    return pl.pallas_call(
        functools.partial(_even_mix_kernel, ts=ts),
        out_shape=jax.ShapeDtypeStruct((s, GLA_V + SGU_W), BF16),
        grid=(s // ts,),
        in_specs=[pl.BlockSpec((nblk, ts, LANES), lambda i: (0, i, 0)),
                  pl.BlockSpec(wg.shape, lambda i: (0, 0)),
                  pl.BlockSpec(bg.shape, lambda i: (0, 0)),
                  pl.BlockSpec(gn.shape, lambda i: (0, 0)),
                  pl.BlockSpec(lng.shape, lambda i: (0, 0)),
                  pl.BlockSpec(lnb.shape, lambda i: (0, 0)),
                  pl.BlockSpec(ws.shape, lambda i: (0, 0, 0)),
                  pl.BlockSpec(bst.shape, lambda i: (0, 0))],
        out_specs=pl.BlockSpec((ts, GLA_V + SGU_W), lambda i: (i, 0)),
        scratch_shapes=[pltpu.VMEM((GLA_HEADS, GLA_DV, GLA_DK), F32),
                        pltpu.VMEM((SGU_GROUPS, SGU_CHUNK, SGU_CHUNK), BF16),
                        pltpu.VMEM((SGU_GROUPS, SGU_CHUNK, LANES), F32)],
        compiler_params=pltpu.CompilerParams(
            dimension_semantics=("arbitrary",), vmem_limit_bytes=VMEM_LIMIT),
        name="even_mix",
    )(h3, wg, bg, gn, lng, lnb, ws, bst)


def _sb_kernel(q_ref, k_ref, v_ref, g_ref, sfx_ref, o_ref, acc_ref, carry_ref, *, blk, nh):
    i = pl.program_id(1)
    r = lax.broadcasted_iota(jnp.int32, (blk, blk), 0)
    c = lax.broadcasted_iota(jnp.int32, (blk, blk), 1)
    strict = c < r

    def visit(hd, j, carry):
        start = pl.multiple_of(j * blk, blk)
        kj = k_ref[hd, pl.ds(start, blk), :]
        vj = v_ref[hd, pl.ds(start, blk), :]
        z2 = _dot_nt(q_ref[hd], kj)
        sp2 = jnp.maximum(z2, 0.0) + jnp.log(1.0 + jnp.exp2(-jnp.abs(z2))) * LOG2E
        addend = jnp.where(strict, sp2, 0.0) if carry is None else sp2
        cum = _dot(addend.astype(BF16), sfx_ref[...])
        x = z2 - cum
        if carry is not None:
            x = x - jnp.concatenate([carry] * (blk // LANES), axis=1)
        w = jnp.exp2(x)
        if carry is None:
            w = jnp.where(strict, w, 0.0)
        pv = _dot(w.astype(BF16), vj)
        return pv, jnp.broadcast_to(cum[:, 0:1], (blk, LANES))

    has_prev = i > 0
    prev = jnp.maximum(i - 1, 0)
    for hd in range(nh):
        pv_d, tot_d = visit(hd, i, None)
        pv_p, tot_p = visit(hd, prev, tot_d)
        acc_ref[hd] = pv_d + jnp.where(has_prev, pv_p, 0.0)
        carry_ref[hd] = tot_d + tot_p

    def more(state):
        j, low = state
        return jnp.logical_and(j >= 0, low < F32_EXP2_UNDERFLOW)

    def body(state):
        j, _ = state
        for hd in range(nh):
            pv, tot = visit(hd, j, carry_ref[hd])
            acc_ref[hd] += pv
            carry_ref[hd] += tot
        return j - 1, jnp.min(carry_ref[...])

    lax.while_loop(more, body, (i - 2, jnp.min(carry_ref[...])))
    for hd in range(nh):
        o_ref[:, hd * SB_DIM:(hd + 1) * SB_DIM] = (
            acc_ref[hd] * _silu(g_ref[hd].astype(F32))).astype(o_ref.dtype)


def _stick_breaking(h3, blk, nh):
    _, s, _ = h3.shape
    groups = SB_HEADS // nh
    rs = lax.broadcasted_iota(jnp.int32, (blk, blk), 0)
    cs = lax.broadcasted_iota(jnp.int32, (blk, blk), 1)
    sfx = jnp.where(rs >= cs, 1.0, 0.0).astype(BF16)
    return pl.pallas_call(
        functools.partial(_sb_kernel, blk=blk, nh=nh),
        out_shape=jax.ShapeDtypeStruct((s, SB_HEADS * SB_DIM), BF16),
        grid=(groups, s // blk),
        in_specs=[pl.BlockSpec((nh, blk, SB_DIM), lambda h, i: (h, i, 0)),
                  pl.BlockSpec((nh, s, SB_DIM), lambda h, i: (groups + h, 0, 0)),
                  pl.BlockSpec((nh, s, SB_DIM), lambda h, i: (2 * groups + h, 0, 0)),
                  pl.BlockSpec((nh, blk, SB_DIM), lambda h, i: (3 * groups + h, i, 0)),
                  pl.BlockSpec(sfx.shape, lambda h, i: (0, 0))],
        out_specs=pl.BlockSpec((blk, nh * SB_DIM), lambda h, i: (i, h)),
        scratch_shapes=[pltpu.VMEM((nh, blk, SB_DIM), F32), pltpu.VMEM((nh, blk, LANES), F32)],
        compiler_params=pltpu.CompilerParams(
            dimension_semantics=("arbitrary", "arbitrary"), vmem_limit_bytes=VMEM_LIMIT),
        name="stick_breaking",
    )(h3, h3, h3, h3, sfx)


def _pick(n, candidates):
    for c in candidates:
        if n % c == 0:
            return c
    raise ValueError(f"no tile of {candidates} divides {n}")


def kernel(x, even_w_in, even_gla_w_gate2, even_gla_b_gate, even_gla_norm_g, even_sgu_ln_g, even_sgu_ln_b,
           even_sgu_w_s, even_sgu_b_s, even_w_out, odd_w_in, odd_w_out, post_ln_g, post_ln_b):
    bsz, seq, d = x.shape
    assert bsz == 1 and d == D_MODEL
    xs = x[0]
    tm = _pick(seq, (1024, 512, 256))
    tl = _pick(seq, (256,))

    w = even_w_in[0]
    o_q, o_k, o_v, o_r = 0, GLA_QK, 2 * GLA_QK, 2 * GLA_QK + GLA_V
    o_g = o_r + GLA_RANK
    w_even = jnp.concatenate(
        [w[:, o_q:o_r], w[:, o_g:], w[:, o_r:o_g], jnp.zeros((d, LANES - GLA_RANK), w.dtype)],
        axis=1).astype(BF16)
    wg = jnp.concatenate([even_gla_w_gate2[0], jnp.zeros((LANES - GLA_RANK, GLA_QK), F32)], axis=0).astype(BF16)
    h3 = _project(xs.astype(BF16), w_even, jnp.ones((1, EVEN_BLOCKS * LANES), F32), tm, 7 * LANES)
    mix = _even_mix(h3, wg, even_gla_b_gate[0][None, :], even_gla_norm_g[0][None, :],
                    even_sgu_ln_g[0][None, :], even_sgu_ln_b[0][None, :],
                    even_sgu_w_s[0], even_sgu_b_s[0].T, tl)
    x1, x1b = _out_proj_ln(mix, even_w_out[0].astype(BF16), xs, post_ln_g[0][None, :], post_ln_b[0][None, :], tl)

    q_scale = jnp.concatenate([jnp.full((1, SB_HEADS * SB_DIM), SB_DIM ** -0.5 * LOG2E, F32),
                               jnp.ones((1, 3 * SB_HEADS * SB_DIM), F32)], axis=1)
    h3 = _project(x1b, odd_w_in[0].astype(BF16), q_scale, tm, 1024)
    att = _stick_breaking(h3, 256, 2)
    x2, _ = _out_proj_ln(att, odd_w_out[0].astype(BF16), x1, post_ln_g[1][None, :], post_ln_b[1][None, :], tl)
    return x2[None]
```

```python
import functools

import jax
import jax.numpy as jnp
from jax import lax
from jax.experimental import pallas as pl
from jax.experimental.pallas import tpu as pltpu

F32 = jnp.float32
BF16 = jnp.bfloat16

LANES = 128
D_MODEL = 2048
DEPTH = 2
GLA_HEADS = 4
GLA_DK = 128
GLA_DV = 256
GLA_QK = GLA_HEADS * GLA_DK
GLA_V = GLA_HEADS * GLA_DV
GLA_RANK = 16
GLA_TAU = 16.0
GLA_CHUNK = 64
SGU_W = 1024
SGU_CHUNK = 128
SGU_GROUPS = SGU_W // LANES
SB_HEADS = 16
SB_DIM = 128
ALPHA = (2.0 * DEPTH) ** 0.25
LN_EPS = 1e-5

_QA, _KA, _VA, _GA, _UB, _VB, _GB, _RA = 0, 4, 8, 16, 24, 32, 40, 48
EVEN_BLOCKS = 49

VMEM_LIMIT = 56 * 1024 * 1024
LOG2E = 1.4426950408889634
F32_EXP2_UNDERFLOW = 151.0


def _softplus(x):
    return jnp.maximum(x, 0.0) + jnp.log(1.0 + jnp.exp(-jnp.abs(x)))


def _silu(x):
    return x / (1.0 + jnp.exp(-x))


def _gelu_tanh(x):
    return 0.5 * x * (1.0 + jnp.tanh(0.7978845608028654 * (x + 0.044715 * (x * x * x))))


def _split_bf16(x):
    hi = x.astype(BF16)
    lo = (x - hi.astype(F32)).astype(BF16)
    return hi, lo


def _dot(a, b):
    return jnp.dot(a, b, preferred_element_type=F32)


def _dot_nt(a, b):
    return lax.dot_general(a, b, (((1,), (1,)), ((), ())), preferred_element_type=F32)


def _proj_kernel(a_ref, w_ref, cs_ref, o_ref):
    r = _dot(a_ref[...], w_ref[...]) * cs_ref[...]
    for c in range(o_ref.shape[0]):
        o_ref[c] = r[:, c * LANES:(c + 1) * LANES].astype(o_ref.dtype)


def _project(a, w, col_scale, tm, tn):
    m, k = a.shape
    n = w.shape[1]
    nb = tn // LANES
    return pl.pallas_call(
        _proj_kernel,
        out_shape=jax.ShapeDtypeStruct((n // LANES, m, LANES), BF16),
        grid=(n // tn, m // tm),
        in_specs=[pl.BlockSpec((tm, k), lambda j, i: (i, 0)),
                  pl.BlockSpec((k, tn), lambda j, i: (0, j)),
                  pl.BlockSpec((1, tn), lambda j, i: (0, j))],
        out_specs=pl.BlockSpec((nb, tm, LANES), lambda j, i: (j, i, 0)),
        compiler_params=pltpu.CompilerParams(
            dimension_semantics=("arbitrary", "arbitrary"), vmem_limit_bytes=VMEM_LIMIT),
        name="in_proj",
    )(a, w, col_scale)


def _out_ln_kernel(a_ref, w_ref, x_ref, g_ref, b_ref, o_ref, ob_ref):
    y = _dot(a_ref[...], w_ref[...])
    r = ALPHA * x_ref[...] + y
    mu = jnp.mean(r, axis=-1, keepdims=True)
    rc = r - mu
    var = jnp.mean(rc * rc, axis=-1, keepdims=True)
    out = rc * lax.rsqrt(var + LN_EPS) * g_ref[...] + b_ref[...]
    o_ref[...] = out
    ob_ref[...] = out.astype(BF16)


def _out_proj_ln(a, w, x, g, b, tm):
    m, k = a.shape
    n = w.shape[1]
    return pl.pallas_call(
        _out_ln_kernel,
        out_shape=(jax.ShapeDtypeStruct((m, n), F32), jax.ShapeDtypeStruct((m, n), BF16)),
        grid=(m // tm,),
        in_specs=[pl.BlockSpec((tm, k), lambda i: (i, 0)),
                  pl.BlockSpec((k, n), lambda i: (0, 0)),
                  pl.BlockSpec((tm, n), lambda i: (i, 0)),
                  pl.BlockSpec((1, n), lambda i: (0, 0)),
                  pl.BlockSpec((1, n), lambda i: (0, 0))],
        out_specs=(pl.BlockSpec((tm, n), lambda i: (i, 0)),
                   pl.BlockSpec((tm, n), lambda i: (i, 0))),
        compiler_params=pltpu.CompilerParams(
            dimension_semantics=("arbitrary",), vmem_limit_bytes=VMEM_LIMIT),
        name="out_proj_ln",
    )(a, w, x, g, b)


def _even_mix_kernel(h_ref, wg_ref, bg_ref, gn_ref, lng_ref, lnb_ref, ws_ref, bst_ref,
                     o_ref, state_ref, wt_ref, bias_ref, *, ts):
    step = pl.program_id(0)

    @pl.when(step == 0)
    def _init():
        state_ref[...] = jnp.zeros_like(state_ref)
        r = lax.broadcasted_iota(jnp.int32, (SGU_CHUNK, SGU_CHUNK), 0)
        c = lax.broadcasted_iota(jnp.int32, (SGU_CHUNK, SGU_CHUNK), 1)
        for g in range(SGU_GROUPS):
            wt_ref[g] = jnp.where(c <= r, ws_ref[g], 0.0).astype(BF16)
            bias_ref[g] = jnp.broadcast_to(bst_ref[:, g:g + 1], (SGU_CHUNK, LANES))

    u = _dot(h_ref[_RA], wg_ref[...]) + bg_ref[...]
    log_a = -_softplus(-u) * (1.0 / GLA_TAU)
    r = lax.broadcasted_iota(jnp.int32, (ts, ts), 0)
    c = lax.broadcasted_iota(jnp.int32, (ts, ts), 1)
    same_chunk = (r // GLA_CHUNK) == (c // GLA_CHUNK)
    cum_mat = jnp.where(same_chunk & (c <= r), 1.0, 0.0).astype(BF16)
    la_hi, la_lo = _split_bf16(log_a)
    b_all = _dot(cum_mat, la_hi) + _dot(cum_mat, la_lo)

    rr = lax.broadcasted_iota(jnp.int32, (GLA_CHUNK, GLA_CHUNK), 0)
    cc = lax.broadcasted_iota(jnp.int32, (GLA_CHUNK, GLA_CHUNK), 1)
    causal = cc <= rr
    scale = GLA_DK ** -0.5
    gn = gn_ref[...]

    for ck in range(ts // GLA_CHUNK):
        rows = slice(ck * GLA_CHUNK, (ck + 1) * GLA_CHUNK)
        bc = b_all[rows]
        b_mid = bc[GLA_CHUNK // 2:GLA_CHUNK // 2 + 1]
        b_last = bc[GLA_CHUNK - 1:GLA_CHUNK]
        e_q = jnp.exp(bc - b_mid) * scale
        e_k = jnp.exp(b_mid - bc)
        e_kv = jnp.exp(b_last - bc)
        e_b = jnp.exp(bc) * scale
        decay = jnp.exp(b_last)
        for hd in range(GLA_HEADS):
            sl = slice(hd * GLA_DK, (hd + 1) * GLA_DK)
            qh = h_ref[_QA + hd, rows, :].astype(F32)
            kh = h_ref[_KA + hd, rows, :].astype(F32)
            vh = jnp.concatenate([h_ref[_VA + 2 * hd, rows, :], h_ref[_VA + 2 * hd + 1, rows, :]], axis=1)
            sc = _dot_nt((qh * e_q[:, sl]).astype(BF16), (kh * e_k[:, sl]).astype(BF16))
            sc = jnp.where(causal, sc, 0.0).astype(BF16)
            o = _dot(sc, vh)
            st = state_ref[hd]
            o = o + _dot_nt((qh * e_b[:, sl]).astype(BF16), st.astype(BF16))
            kk = (kh * e_kv[:, sl]).astype(BF16)
            kv_t = lax.dot_general(vh, kk, (((0,), (0,)), ((), ())), preferred_element_type=F32)
            state_ref[hd] = st * decay[:, sl] + kv_t
            ms = jnp.mean(o * o, axis=-1, keepdims=True)
            on = o * lax.rsqrt(ms + LN_EPS) * gn
            ga = jnp.concatenate([h_ref[_GA + 2 * hd, rows, :], h_ref[_GA + 2 * hd + 1, rows, :]],
                                 axis=1).astype(F32)
            o_ref[rows, hd * GLA_DV:(hd + 1) * GLA_DV] = (on * _silu(ga)).astype(o_ref.dtype)

    vg = [_gelu_tanh(h_ref[_VB + g].astype(F32)) for g in range(SGU_GROUPS)]
    tot = vg[0]
    for g in range(1, SGU_GROUPS):
        tot = tot + vg[g]
    mu = jnp.sum(tot, axis=-1, keepdims=True) * (1.0 / SGU_W)
    vc = [v - mu for v in vg]
    sq = vc[0] * vc[0]
    for g in range(1, SGU_GROUPS):
        sq = sq + vc[g] * vc[g]
    rstd = lax.rsqrt(jnp.sum(sq, axis=-1, keepdims=True) * (1.0 / SGU_W) + LN_EPS)
    for g in range(SGU_GROUPS):
        cols = slice(g * LANES, (g + 1) * LANES)
        vn = (vc[g] * rstd * lng_ref[:, cols] + lnb_ref[:, cols]).astype(BF16)
        ug = _gelu_tanh(h_ref[_UB + g].astype(F32))
        gate = _silu(h_ref[_GB + g].astype(F32))
        for ck in range(ts // SGU_CHUNK):
            rows = slice(ck * SGU_CHUNK, (ck + 1) * SGU_CHUNK)
            s = _dot(wt_ref[g], vn[rows]) + bias_ref[g]
            o_ref[rows, GLA_V + g * LANES:GLA_V + (g + 1) * LANES] = (
                ug[rows] * s * gate[rows]).astype(o_ref.dtype)


def _even_mix(h3, wg, bg, gn, lng, lnb, ws, bst, ts):
    nblk, s, _ = h3.shape
    return pl.pallas_call(
        functools.partial(_even_mix_kernel, ts=ts),
        out_shape=jax.ShapeDtypeStruct((s, GLA_V + SGU_W), BF16),
        grid=(s // ts,),
        in_specs=[pl.BlockSpec((nblk, ts, LANES), lambda i: (0, i, 0)),
                  pl.BlockSpec(wg.shape, lambda i: (0, 0)),
                  pl.BlockSpec(bg.shape, lambda i: (0, 0)),
                  pl.BlockSpec(gn.shape, lambda i: (0, 0)),
                  pl.BlockSpec(lng.shape, lambda i: (0, 0)),
                  pl.BlockSpec(lnb.shape, lambda i: (0, 0)),
                  pl.BlockSpec(ws.shape, lambda i: (0, 0, 0)),
                  pl.BlockSpec(bst.shape, lambda i: (0, 0))],
        out_specs=pl.BlockSpec((ts, GLA_V + SGU_W), lambda i: (i, 0)),
        scratch_shapes=[pltpu.VMEM((GLA_HEADS, GLA_DV, GLA_DK), F32),
                        pltpu.VMEM((SGU_GROUPS, SGU_CHUNK, SGU_CHUNK), BF16),
                        pltpu.VMEM((SGU_GROUPS, SGU_CHUNK, LANES), F32)],
        compiler_params=pltpu.CompilerParams(
            dimension_semantics=("arbitrary",), vmem_limit_bytes=VMEM_LIMIT),
        name="even_mix",
    )(h3, wg, bg, gn, lng, lnb, ws, bst)


def _sb_kernel(q_ref, k_ref, v_ref, g_ref, sfx_ref, o_ref, acc_ref, carry_ref, *, blk, nh):
    i = pl.program_id(1)
    r = lax.broadcasted_iota(jnp.int32, (blk, blk), 0)
    c = lax.broadcasted_iota(jnp.int32, (blk, blk), 1)
    strict = c < r

    def visit(hd, j, carry):
        start = pl.multiple_of(j * blk, blk)
        kj = k_ref[hd, pl.ds(start, blk), :]
        vj = v_ref[hd, pl.ds(start, blk), :]
        z2 = _dot_nt(q_ref[hd], kj)
        sp2 = jnp.maximum(z2, 0.0) + jnp.log(1.0 + jnp.exp2(-jnp.abs(z2))) * LOG2E
        addend = jnp.where(strict, sp2, 0.0) if carry is None else sp2
        cum = _dot(addend.astype(BF16), sfx_ref[...])
        x = z2 - cum
        if carry is not None:
            x = x - jnp.concatenate([carry] * (blk // LANES), axis=1)
        w = jnp.exp2(x)
        if carry is None:
            w = jnp.where(strict, w, 0.0)
        pv = _dot(w.astype(BF16), vj)
        return pv, jnp.broadcast_to(cum[:, 0:1], (blk, LANES))

    has_prev = i > 0
    prev = jnp.maximum(i - 1, 0)
    sd = pl.multiple_of(i * blk, blk)
    sp_ = pl.multiple_of(prev * blk, blk)
    z_d = [_dot_nt(q_ref[hd], k_ref[hd, pl.ds(sd, blk), :]) for hd in range(nh)]
    z_p = [_dot_nt(q_ref[hd], k_ref[hd, pl.ds(sp_, blk), :]) for hd in range(nh)]

    def softplus2(z2):
        return jnp.maximum(z2, 0.0) + jnp.log(1.0 + jnp.exp2(-jnp.abs(z2))) * LOG2E

    a_d = [jnp.where(strict, softplus2(z), 0.0).astype(BF16) for z in z_d]
    a_p = [softplus2(z).astype(BF16) for z in z_p]
    cum = _dot(jnp.concatenate(a_d + a_p, axis=0), sfx_ref[...])
    for hd in range(nh):
        c_d = cum[hd * blk:(hd + 1) * blk]
        c_p = cum[(nh + hd) * blk:(nh + hd + 1) * blk]
        tot_d = jnp.broadcast_to(c_d[:, 0:1], (blk, LANES))
        tot_p = jnp.broadcast_to(c_p[:, 0:1], (blk, LANES))
        w_d = jnp.where(strict, jnp.exp2(z_d[hd] - c_d), 0.0).astype(BF16)
        w_p = jnp.exp2(z_p[hd] - c_p - jnp.concatenate([tot_d] * (blk // LANES), axis=1)).astype(BF16)
        pv_d = _dot(w_d, v_ref[hd, pl.ds(sd, blk), :])
        pv_p = _dot(w_p, v_ref[hd, pl.ds(sp_, blk), :])
        acc_ref[hd] = pv_d + jnp.where(has_prev, pv_p, 0.0)
        carry_ref[hd] = tot_d + tot_p

    def more(state):
        j, low = state
        return jnp.logical_and(j >= 0, low < F32_EXP2_UNDERFLOW)

    def body(state):
        j, _ = state
        for hd in range(nh):
            pv, tot = visit(hd, j, carry_ref[hd])
            acc_ref[hd] += pv
            carry_ref[hd] += tot
        return j - 1, jnp.min(carry_ref[...])

    lax.while_loop(more, body, (i - 2, jnp.min(carry_ref[...])))
    for hd in range(nh):
        o_ref[:, hd * SB_DIM:(hd + 1) * SB_DIM] = (
            acc_ref[hd] * _silu(g_ref[hd].astype(F32))).astype(o_ref.dtype)


def _stick_breaking(h3, blk, nh):
    _, s, _ = h3.shape
    groups = SB_HEADS // nh
    rs = lax.broadcasted_iota(jnp.int32, (blk, blk), 0)
    cs = lax.broadcasted_iota(jnp.int32, (blk, blk), 1)
    sfx = jnp.where(rs >= cs, 1.0, 0.0).astype(BF16)
    return pl.pallas_call(
        functools.partial(_sb_kernel, blk=blk, nh=nh),
        out_shape=jax.ShapeDtypeStruct((s, SB_HEADS * SB_DIM), BF16),
        grid=(groups, s // blk),
        in_specs=[pl.BlockSpec((nh, blk, SB_DIM), lambda h, i: (h, i, 0)),
                  pl.BlockSpec((nh, s, SB_DIM), lambda h, i: (groups + h, 0, 0), pipeline_mode=pl.Buffered(1)),
                  pl.BlockSpec((nh, s, SB_DIM), lambda h, i: (2 * groups + h, 0, 0), pipeline_mode=pl.Buffered(1)),
                  pl.BlockSpec((nh, blk, SB_DIM), lambda h, i: (3 * groups + h, i, 0)),
                  pl.BlockSpec(sfx.shape, lambda h, i: (0, 0))],
        out_specs=pl.BlockSpec((blk, nh * SB_DIM), lambda h, i: (i, h)),
        scratch_shapes=[pltpu.VMEM((nh, blk, SB_DIM), F32), pltpu.VMEM((nh, blk, LANES), F32)],
        compiler_params=pltpu.CompilerParams(
            dimension_semantics=("arbitrary", "arbitrary"), vmem_limit_bytes=VMEM_LIMIT),
        name="stick_breaking",
    )(h3, h3, h3, h3, sfx)


def _pick(n, candidates):
    for c in candidates:
        if n % c == 0:
            return c
    raise ValueError(f"no tile of {candidates} divides {n}")


def kernel(x, even_w_in, even_gla_w_gate2, even_gla_b_gate, even_gla_norm_g, even_sgu_ln_g, even_sgu_ln_b,
           even_sgu_w_s, even_sgu_b_s, even_w_out, odd_w_in, odd_w_out, post_ln_g, post_ln_b):
    bsz, seq, d = x.shape
    assert bsz == 1 and d == D_MODEL
    xs = x[0]
    tm = _pick(seq, (1024, 512, 256))
    tl = _pick(seq, (256,))

    w = even_w_in[0]
    o_q, o_k, o_v, o_r = 0, GLA_QK, 2 * GLA_QK, 2 * GLA_QK + GLA_V
    o_g = o_r + GLA_RANK
    w_even = jnp.concatenate(
        [w[:, o_q:o_r], w[:, o_g:], w[:, o_r:o_g], jnp.zeros((d, LANES - GLA_RANK), w.dtype)],
        axis=1).astype(BF16)
    wg = jnp.concatenate([even_gla_w_gate2[0], jnp.zeros((LANES - GLA_RANK, GLA_QK), F32)], axis=0).astype(BF16)
    h3 = _project(xs.astype(BF16), w_even, jnp.ones((1, EVEN_BLOCKS * LANES), F32), tm, 7 * LANES)
    mix = _even_mix(h3, wg, even_gla_b_gate[0][None, :], even_gla_norm_g[0][None, :],
                    even_sgu_ln_g[0][None, :], even_sgu_ln_b[0][None, :],
                    even_sgu_w_s[0], even_sgu_b_s[0].T, tl)
    x1, x1b = _out_proj_ln(mix, even_w_out[0].astype(BF16), xs, post_ln_g[0][None, :], post_ln_b[0][None, :], tl)

    q_scale = jnp.concatenate([jnp.full((1, SB_HEADS * SB_DIM), SB_DIM ** -0.5 * LOG2E, F32),
                               jnp.ones((1, 3 * SB_HEADS * SB_DIM), F32)], axis=1)
    h3 = _project(x1b, odd_w_in[0].astype(BF16), q_scale, tm, 1024)
    att = _stick_breaking(h3, 256, 4)
    x2, _ = _out_proj_ln(att, odd_w_out[0].astype(BF16), x1, post_ln_g[1][None, :], post_ln_b[1][None, :], tl)
    return x2[None]
```

```python
import functools

import jax
import jax.numpy as jnp
from jax import lax
from jax.experimental import pallas as pl
from jax.experimental.pallas import tpu as pltpu

F32 = jnp.float32
BF16 = jnp.bfloat16

LANES = 128
D_MODEL = 2048
DEPTH = 2
GLA_HEADS = 4
GLA_DK = 128
GLA_DV = 256
GLA_QK = GLA_HEADS * GLA_DK
GLA_V = GLA_HEADS * GLA_DV
GLA_RANK = 16
GLA_TAU = 16.0
GLA_CHUNK = 64
SGU_W = 1024
SGU_CHUNK = 128
SGU_GROUPS = SGU_W // LANES
SB_HEADS = 16
SB_DIM = 128
ALPHA = (2.0 * DEPTH) ** 0.25
LN_EPS = 1e-5

_QA, _KA, _VA, _GA, _UB, _VB, _GB, _RA = 0, 4, 8, 16, 24, 32, 40, 48
EVEN_BLOCKS = 50
EVEN_TN = 1280

VMEM_LIMIT = 56 * 1024 * 1024
LOG2E = 1.4426950408889634
F32_EXP2_UNDERFLOW = 151.0


def _softplus(x):
    return jnp.maximum(x, 0.0) + jnp.log(1.0 + jnp.exp(-jnp.abs(x)))


def _silu(x):
    return x / (1.0 + jnp.exp2(x * -LOG2E))


def _gelu_tanh(x):
    k1 = -2.0 * 0.7978845608028654 * LOG2E
    return x / (1.0 + jnp.exp2(x * (k1 + (k1 * 0.044715) * (x * x))))


def _split_bf16(x):
    hi = x.astype(BF16)
    lo = (x - hi.astype(F32)).astype(BF16)
    return hi, lo


def _dot(a, b):
    return jnp.dot(a, b, preferred_element_type=F32)


def _dot_nt(a, b):
    return lax.dot_general(a, b, (((1,), (1,)), ((), ())), preferred_element_type=F32)


def _proj_kernel(a_ref, w_ref, cs_ref, o_ref, ab_ref, *, w_transposed):
    @pl.when(pl.program_id(1) == 0)
    def _cast():
        ab_ref[...] = a_ref[...].astype(BF16)

    dot = _dot_nt if w_transposed else _dot
    r = dot(ab_ref[...], w_ref[...]) * cs_ref[...]
    for c in range(o_ref.shape[0]):
        o_ref[c] = r[:, c * LANES:(c + 1) * LANES].astype(o_ref.dtype)


def _project(a, w, col_scale, tm, tn, w_transposed):
    m, k = a.shape
    n = w.shape[0] if w_transposed else w.shape[1]
    nb = tn // LANES
    w_spec = (pl.BlockSpec((tn, k), lambda i, j: (j, 0)) if w_transposed
              else pl.BlockSpec((k, tn), lambda i, j: (0, j)))
    return pl.pallas_call(
        functools.partial(_proj_kernel, w_transposed=w_transposed),
        out_shape=jax.ShapeDtypeStruct((n // LANES, m, LANES), BF16),
        grid=(m // tm, n // tn),
        in_specs=[pl.BlockSpec((tm, k), lambda i, j: (i, 0)),
                  w_spec,
                  pl.BlockSpec((1, tn), lambda i, j: (0, j))],
        out_specs=pl.BlockSpec((nb, tm, LANES), lambda i, j: (j, i, 0)),
        scratch_shapes=[pltpu.VMEM((tm, k), BF16)],
        compiler_params=pltpu.CompilerParams(
            dimension_semantics=("arbitrary", "arbitrary"), vmem_limit_bytes=VMEM_LIMIT),
        name="in_proj",
    )(a, w, col_scale)


def _out_ln_kernel(a_ref, w_ref, x_ref, g_ref, b_ref, o_ref):
    y = _dot(a_ref[...], w_ref[...])
    r = ALPHA * x_ref[...] + y
    mu = jnp.mean(r, axis=-1, keepdims=True)
    rc = r - mu
    var = jnp.mean(rc * rc, axis=-1, keepdims=True)
    o_ref[...] = rc * lax.rsqrt(var + LN_EPS) * g_ref[...] + b_ref[...]


def _out_proj_ln(a, w, x, g, b, tm):
    m, k = a.shape
    n = w.shape[1]
    return pl.pallas_call(
        _out_ln_kernel,
        out_shape=jax.ShapeDtypeStruct((m, n), F32),
        grid=(m // tm,),
        in_specs=[pl.BlockSpec((tm, k), lambda i: (i, 0)),
                  pl.BlockSpec((k, n), lambda i: (0, 0)),
                  pl.BlockSpec((tm, n), lambda i: (i, 0)),
                  pl.BlockSpec((1, n), lambda i: (0, 0)),
                  pl.BlockSpec((1, n), lambda i: (0, 0))],
        out_specs=pl.BlockSpec((tm, n), lambda i: (i, 0)),
        compiler_params=pltpu.CompilerParams(
            dimension_semantics=("arbitrary",), vmem_limit_bytes=VMEM_LIMIT),
        name="out_proj_ln",
    )(a, w, x, g, b)


def _even_mix_kernel(h_ref, wg_ref, bg_ref, gn_ref, lng_ref, lnb_ref, ws_ref, bst_ref,
                     o_ref, state_ref, wt_ref, bias_ref, *, ts):
    step = pl.program_id(0)

    @pl.when(step == 0)
    def _init():
        state_ref[...] = jnp.zeros_like(state_ref)
        r = lax.broadcasted_iota(jnp.int32, (SGU_CHUNK, SGU_CHUNK), 0)
        c = lax.broadcasted_iota(jnp.int32, (SGU_CHUNK, SGU_CHUNK), 1)
        for g in range(SGU_GROUPS):
            wt_ref[g] = jnp.where(c <= r, ws_ref[g], 0.0).astype(BF16)
            bias_ref[g] = jnp.broadcast_to(bst_ref[:, g:g + 1], (SGU_CHUNK, LANES))

    u = _dot(h_ref[_RA], wg_ref[...]) + bg_ref[...]
    log_a = -_softplus(-u) * (1.0 / GLA_TAU)
    r = lax.broadcasted_iota(jnp.int32, (ts, ts), 0)
    c = lax.broadcasted_iota(jnp.int32, (ts, ts), 1)
    same_chunk = (r // GLA_CHUNK) == (c // GLA_CHUNK)
    cum_mat = jnp.where(same_chunk & (c <= r), 1.0, 0.0).astype(BF16)
    la_hi, la_lo = _split_bf16(log_a)
    b_all = _dot(cum_mat, la_hi) + _dot(cum_mat, la_lo)

    rr = lax.broadcasted_iota(jnp.int32, (GLA_CHUNK, GLA_CHUNK), 0)
    cc = lax.broadcasted_iota(jnp.int32, (GLA_CHUNK, GLA_CHUNK), 1)
    causal = cc <= rr
    scale = GLA_DK ** -0.5
    gn = gn_ref[...]

    for ck in range(ts // GLA_CHUNK):
        rows = slice(ck * GLA_CHUNK, (ck + 1) * GLA_CHUNK)
        bc = b_all[rows]
        b_mid = bc[GLA_CHUNK // 2:GLA_CHUNK // 2 + 1]
        b_last = bc[GLA_CHUNK - 1:GLA_CHUNK]
        e_q = jnp.exp(bc - b_mid) * scale
        e_k = jnp.exp(b_mid - bc)
        e_kv = jnp.exp(b_last - bc)
        e_b = jnp.exp(bc) * scale
        decay = jnp.exp(b_last)
        for hd in range(GLA_HEADS):
            sl = slice(hd * GLA_DK, (hd + 1) * GLA_DK)
            qh = h_ref[_QA + hd, rows, :].astype(F32)
            kh = h_ref[_KA + hd, rows, :].astype(F32)
            vh = jnp.concatenate([h_ref[_VA + 2 * hd, rows, :], h_ref[_VA + 2 * hd + 1, rows, :]], axis=1)
            sc = _dot_nt((qh * e_q[:, sl]).astype(BF16), (kh * e_k[:, sl]).astype(BF16))
            sc = jnp.where(causal, sc, 0.0).astype(BF16)
            o = _dot(sc, vh)
            st = state_ref[hd]
            o = o + _dot_nt((qh * e_b[:, sl]).astype(BF16), st.astype(BF16))
            kk = (kh * e_kv[:, sl]).astype(BF16)
            kv_t = lax.dot_general(vh, kk, (((0,), (0,)), ((), ())), preferred_element_type=F32)
            state_ref[hd] = st * decay[:, sl] + kv_t
            ms = jnp.mean(o * o, axis=-1, keepdims=True)
            on = o * lax.rsqrt(ms + LN_EPS) * gn
            ga = jnp.concatenate([h_ref[_GA + 2 * hd, rows, :], h_ref[_GA + 2 * hd + 1, rows, :]],
                                 axis=1).astype(F32)
            o_ref[rows, hd * GLA_DV:(hd + 1) * GLA_DV] = (on * _silu(ga)).astype(o_ref.dtype)

    vg = [_gelu_tanh(h_ref[_VB + g].astype(F32)) for g in range(SGU_GROUPS)]
    tot = vg[0]
    for g in range(1, SGU_GROUPS):
        tot = tot + vg[g]
    mu = jnp.sum(tot, axis=-1, keepdims=True) * (1.0 / SGU_W)
    vc = [v - mu for v in vg]
    sq = vc[0] * vc[0]
    for g in range(1, SGU_GROUPS):
        sq = sq + vc[g] * vc[g]
    rstd = lax.rsqrt(jnp.sum(sq, axis=-1, keepdims=True) * (1.0 / SGU_W) + LN_EPS)
    for g in range(SGU_GROUPS):
        cols = slice(g * LANES, (g + 1) * LANES)
        vn = (vc[g] * rstd * lng_ref[:, cols] + lnb_ref[:, cols]).astype(BF16)
        ug = _gelu_tanh(h_ref[_UB + g].astype(F32))
        gate = _silu(h_ref[_GB + g].astype(F32))
        for ck in range(ts // SGU_CHUNK):
            rows = slice(ck * SGU_CHUNK, (ck + 1) * SGU_CHUNK)
            s = _dot(wt_ref[g], vn[rows]) + bias_ref[g]
            o_ref[rows, GLA_V + g * LANES:GLA_V + (g + 1) * LANES] = (
                ug[rows] * s * gate[rows]).astype(o_ref.dtype)


def _even_mix(h3, wg, bg, gn, lng, lnb, ws, bst, ts):
    nblk, s, _ = h3.shape
    return pl.pallas_call(
        functools.partial(_even_mix_kernel, ts=ts),
        out_shape=jax.ShapeDtypeStruct((s, GLA_V + SGU_W), BF16),
        grid=(s // ts,),
        in_specs=[pl.BlockSpec((nblk, ts, LANES), lambda i: (0, i, 0)),
                  pl.BlockSpec(wg.shape, lambda i: (0, 0)),
                  pl.BlockSpec(bg.shape, lambda i: (0, 0)),
                  pl.BlockSpec(gn.shape, lambda i: (0, 0)),
                  pl.BlockSpec(lng.shape, lambda i: (0, 0)),
                  pl.BlockSpec(lnb.shape, lambda i: (0, 0)),
                  pl.BlockSpec(ws.shape, lambda i: (0, 0, 0)),
                  pl.BlockSpec(bst.shape, lambda i: (0, 0))],
        out_specs=pl.BlockSpec((ts, GLA_V + SGU_W), lambda i: (i, 0)),
        scratch_shapes=[pltpu.VMEM((GLA_HEADS, GLA_DV, GLA_DK), F32),
                        pltpu.VMEM((SGU_GROUPS, SGU_CHUNK, SGU_CHUNK), BF16),
                        pltpu.VMEM((SGU_GROUPS, SGU_CHUNK, LANES), F32)],
        compiler_params=pltpu.CompilerParams(
            dimension_semantics=("arbitrary",), vmem_limit_bytes=VMEM_LIMIT),
        name="even_mix",
    )(h3, wg, bg, gn, lng, lnb, ws, bst)


def _sb_kernel(q_ref, k_ref, v_ref, g_ref, sfx_ref, o_ref, acc_ref, carry_ref, *, blk, nh):
    i = pl.program_id(1)
    r = lax.broadcasted_iota(jnp.int32, (blk, blk), 0)
    c = lax.broadcasted_iota(jnp.int32, (blk, blk), 1)
    strict = c < r

    def visit(hd, j, carry):
        start = pl.multiple_of(j * blk, blk)
        kj = k_ref[hd, pl.ds(start, blk), :]
        vj = v_ref[hd, pl.ds(start, blk), :]
        z2 = _dot_nt(q_ref[hd], kj)
        sp2 = jnp.maximum(z2, 0.0) + jnp.log(1.0 + jnp.exp2(-jnp.abs(z2))) * LOG2E
        addend = jnp.where(strict, sp2, 0.0) if carry is None else sp2
        cum = _dot(addend.astype(BF16), sfx_ref[...])
        x = z2 - cum
        if carry is not None:
            x = x - jnp.concatenate([carry] * (blk // LANES), axis=1)
        w = jnp.exp2(x)
        if carry is None:
            w = jnp.where(strict, w, 0.0)
        pv = _dot(w.astype(BF16), vj)
        return pv, jnp.broadcast_to(cum[:, 0:1], (blk, LANES))

    has_prev = i > 0
    prev = jnp.maximum(i - 1, 0)
    sd = pl.multiple_of(i * blk, blk)
    sp_ = pl.multiple_of(prev * blk, blk)
    z_d = [_dot_nt(q_ref[hd], k_ref[hd, pl.ds(sd, blk), :]) for hd in range(nh)]
    z_p = [_dot_nt(q_ref[hd], k_ref[hd, pl.ds(sp_, blk), :]) for hd in range(nh)]

    def softplus2(z2):
        return jnp.maximum(z2, 0.0) + jnp.log(1.0 + jnp.exp2(-jnp.abs(z2))) * LOG2E

    a_d = [jnp.where(strict, softplus2(z), 0.0).astype(BF16) for z in z_d]
    a_p = [softplus2(z).astype(BF16) for z in z_p]
    cum = _dot(jnp.concatenate(a_d + a_p, axis=0), sfx_ref[...])
    for hd in range(nh):
        c_d = cum[hd * blk:(hd + 1) * blk]
        c_p = cum[(nh + hd) * blk:(nh + hd + 1) * blk]
        tot_d = jnp.broadcast_to(c_d[:, 0:1], (blk, LANES))
        tot_p = jnp.broadcast_to(c_p[:, 0:1], (blk, LANES))
        w_d = jnp.where(strict, jnp.exp2(z_d[hd] - c_d), 0.0).astype(BF16)
        w_p = jnp.exp2(z_p[hd] - c_p - jnp.concatenate([tot_d] * (blk // LANES), axis=1)).astype(BF16)
        pv_d = _dot(w_d, v_ref[hd, pl.ds(sd, blk), :])
        pv_p = _dot(w_p, v_ref[hd, pl.ds(sp_, blk), :])
        acc_ref[hd] = pv_d + jnp.where(has_prev, pv_p, 0.0)
        carry_ref[hd] = tot_d + tot_p

    def more(state):
        j, low = state
        return jnp.logical_and(j >= 0, low < F32_EXP2_UNDERFLOW)

    def body(state):
        j, _ = state
        for hd in range(nh):
            pv, tot = visit(hd, j, carry_ref[hd])
            acc_ref[hd] += pv
            carry_ref[hd] += tot
        return j - 1, jnp.min(carry_ref[...])

    lax.while_loop(more, body, (i - 2, jnp.min(carry_ref[...])))
    for hd in range(nh):
        o_ref[:, hd * SB_DIM:(hd + 1) * SB_DIM] = (
            acc_ref[hd] * _silu(g_ref[hd].astype(F32))).astype(o_ref.dtype)


def _stick_breaking(h3, blk, nh):
    _, s, _ = h3.shape
    groups = SB_HEADS // nh
    rs = lax.broadcasted_iota(jnp.int32, (blk, blk), 0)
    cs = lax.broadcasted_iota(jnp.int32, (blk, blk), 1)
    sfx = jnp.where(rs >= cs, 1.0, 0.0).astype(BF16)
    return pl.pallas_call(
        functools.partial(_sb_kernel, blk=blk, nh=nh),
        out_shape=jax.ShapeDtypeStruct((s, SB_HEADS * SB_DIM), BF16),
        grid=(groups, s // blk),
        in_specs=[pl.BlockSpec((nh, blk, SB_DIM), lambda h, i: (h, i, 0)),
                  pl.BlockSpec((nh, s, SB_DIM), lambda h, i: (groups + h, 0, 0), pipeline_mode=pl.Buffered(1)),
                  pl.BlockSpec((nh, s, SB_DIM), lambda h, i: (2 * groups + h, 0, 0), pipeline_mode=pl.Buffered(1)),
                  pl.BlockSpec((nh, blk, SB_DIM), lambda h, i: (3 * groups + h, i, 0)),
                  pl.BlockSpec(sfx.shape, lambda h, i: (0, 0))],
        out_specs=pl.BlockSpec((blk, nh * SB_DIM), lambda h, i: (i, h)),
        scratch_shapes=[pltpu.VMEM((nh, blk, SB_DIM), F32), pltpu.VMEM((nh, blk, LANES), F32)],
        compiler_params=pltpu.CompilerParams(
            dimension_semantics=("arbitrary", "arbitrary"), vmem_limit_bytes=VMEM_LIMIT),
        name="stick_breaking",
    )(h3, h3, h3, h3, sfx)


def _pick(n, candidates):
    for c in candidates:
        if n % c == 0:
            return c
    raise ValueError(f"no tile of {candidates} divides {n}")


def kernel(x, even_w_in, even_gla_w_gate2, even_gla_b_gate, even_gla_norm_g, even_sgu_ln_g, even_sgu_ln_b,
           even_sgu_w_s, even_sgu_b_s, even_w_out, odd_w_in, odd_w_out, post_ln_g, post_ln_b):
    bsz, seq, d = x.shape
    assert bsz == 1 and d == D_MODEL
    xs = x[0]
    tm = _pick(seq, (1024, 512, 256))
    tl = _pick(seq, (256,))
    to = _pick(seq, (512, 256))

    wt = jnp.swapaxes(even_w_in[0], 0, 1)
    o_r = 2 * GLA_QK + GLA_V
    o_g = o_r + GLA_RANK
    n_even = EVEN_BLOCKS * LANES
    w_even_t = jnp.concatenate(
        [wt[:o_r], wt[o_g:], wt[o_r:o_g], jnp.zeros((n_even - wt.shape[0], d), wt.dtype)],
        axis=0).astype(BF16)
    wg = jnp.concatenate([even_gla_w_gate2[0], jnp.zeros((LANES - GLA_RANK, GLA_QK), F32)], axis=0).astype(BF16)
    h3 = _project(xs, w_even_t, jnp.ones((1, n_even), F32), tm, EVEN_TN, True)
    mix = _even_mix(h3, wg, even_gla_b_gate[0][None, :], even_gla_norm_g[0][None, :],
                    even_sgu_ln_g[0][None, :], even_sgu_ln_b[0][None, :],
                    even_sgu_w_s[0], even_sgu_b_s[0].T, tl)
    x1 = _out_proj_ln(mix, even_w_out[0].astype(BF16), xs, post_ln_g[0][None, :], post_ln_b[0][None, :], to)

    q_scale = jnp.concatenate([jnp.full((1, SB_HEADS * SB_DIM), SB_DIM ** -0.5 * LOG2E, F32),
                               jnp.ones((1, 3 * SB_HEADS * SB_DIM), F32)], axis=1)
    h3 = _project(x1, odd_w_in[0].astype(BF16), q_scale, tm, 1024, False)
    att = _stick_breaking(h3, 256, 4)
    x2 = _out_proj_ln(att, odd_w_out[0].astype(BF16), x1, post_ln_g[1][None, :], post_ln_b[1][None, :], to)
    return x2[None]
```

```python
import functools

import jax
import jax.numpy as jnp
from jax import lax
from jax.experimental import pallas as pl
from jax.experimental.pallas import tpu as pltpu

F32 = jnp.float32
BF16 = jnp.bfloat16

LANES = 128
BF16_SUBLANES = 16
D_MODEL = 2048
DEPTH = 2
GLA_HEADS = 4
GLA_DK = 128
GLA_DV = 256
GLA_QK = GLA_HEADS * GLA_DK
GLA_V = GLA_HEADS * GLA_DV
GLA_RANK = 16
GLA_TAU = 16.0
GLA_CHUNK = 64
SGU_W = 1024
SGU_CHUNK = 128
SGU_GROUPS = SGU_W // LANES
SB_HEADS = 16
SB_DIM = 128
ALPHA = (2.0 * DEPTH) ** 0.25
LN_EPS = 1e-5

_QA, _KA, _VA, _GA, _UB, _VB, _GB = 0, 4, 8, 16, 24, 32, 40
EVEN_BLOCKS = 48
OUT_SUB = 128
EVEN_TN = 1024

VMEM_LIMIT = 56 * 1024 * 1024
LOG2E = 1.4426950408889634
F32_EXP2_UNDERFLOW = 151.0


def _softplus(x):
    return jnp.maximum(x, 0.0) + jnp.log(1.0 + jnp.exp(-jnp.abs(x)))


def _silu(x):
    return x / (1.0 + jnp.exp2(x * -LOG2E))


def _gelu_tanh(x):
    k1 = -2.0 * 0.7978845608028654 * LOG2E
    return x / (1.0 + jnp.exp2(x * (k1 + (k1 * 0.044715) * (x * x))))


def _split_bf16(x):
    hi = x.astype(BF16)
    lo = (x - hi.astype(F32)).astype(BF16)
    return hi, lo


def _dot(a, b):
    return jnp.dot(a, b, preferred_element_type=F32)


def _dot_nt(a, b):
    return lax.dot_general(a, b, (((1,), (1,)), ((), ())), preferred_element_type=F32)


def _proj_kernel(*refs, w_transposed, has_side):
    if has_side:
        a_ref, w_ref, cs_ref, ws_ref, o_ref, os_ref, ab_ref = refs
    else:
        a_ref, w_ref, cs_ref, o_ref, ab_ref = refs

    @pl.when(pl.program_id(1) == 0)
    def _first():
        ab_ref[...] = a_ref[...].astype(BF16)
        if has_side:
            os_ref[0] = _dot_nt(ab_ref[...], ws_ref[...]).astype(os_ref.dtype)

    dot = _dot_nt if w_transposed else _dot
    r = dot(ab_ref[...], w_ref[...]) * cs_ref[...]
    for c in range(o_ref.shape[0]):
        o_ref[c] = r[:, c * LANES:(c + 1) * LANES].astype(o_ref.dtype)


def _project(a, w, col_scale, tm, tn, w_transposed, row_start=None, side_w=None):
    m, k = a.shape
    n = col_scale.shape[1]
    nb = tn // LANES
    if not w_transposed:
        w_spec = pl.BlockSpec((k, tn), lambda i, j: (0, j))
    elif row_start is None:
        w_spec = pl.BlockSpec((tn, k), lambda i, j: (j, 0))
    else:
        w_spec = pl.BlockSpec((pl.Element(tn), pl.Element(k)),
                              lambda i, j: (pl.multiple_of(row_start(j), BF16_SUBLANES), 0))
    in_specs = [pl.BlockSpec((tm, k), lambda i, j: (i, 0)), w_spec, pl.BlockSpec((1, tn), lambda i, j: (0, j))]
    out_shape = [jax.ShapeDtypeStruct((n // LANES, m, LANES), BF16)]
    out_specs = [pl.BlockSpec((nb, tm, LANES), lambda i, j: (j, i, 0))]
    operands = [a, w, col_scale]
    if side_w is not None:
        in_specs.append(pl.BlockSpec(side_w.shape, lambda i, j: (0, 0)))
        out_shape.append(jax.ShapeDtypeStruct((1, m, LANES), BF16))
        out_specs.append(pl.BlockSpec((1, tm, LANES), lambda i, j: (0, i, 0)))
        operands.append(side_w)
    return pl.pallas_call(
        functools.partial(_proj_kernel, w_transposed=w_transposed, has_side=side_w is not None),
        out_shape=out_shape,
        grid=(m // tm, n // tn),
        in_specs=in_specs,
        out_specs=out_specs,
        scratch_shapes=[pltpu.VMEM((tm, k), BF16)],
        compiler_params=pltpu.CompilerParams(
            dimension_semantics=("arbitrary", "arbitrary"), vmem_limit_bytes=VMEM_LIMIT),
        name="in_proj",
    )(*operands)


def _out_ln_kernel(a_ref, w_ref, x_ref, g_ref, b_ref, o_ref, *, sub):
    for t in range(o_ref.shape[0] // sub):
        rows = slice(t * sub, (t + 1) * sub)
        r = ALPHA * x_ref[rows, :] + _dot(a_ref[rows, :], w_ref[...])
        mu = jnp.mean(r, axis=-1, keepdims=True)
        rc = r - mu
        var = jnp.mean(rc * rc, axis=-1, keepdims=True)
        o_ref[rows, :] = rc * lax.rsqrt(var + LN_EPS) * g_ref[...] + b_ref[...]


def _out_proj_ln(a, w, x, g, b, tm, sub):
    m, k = a.shape
    n = w.shape[1]
    return pl.pallas_call(
        functools.partial(_out_ln_kernel, sub=sub),
        out_shape=jax.ShapeDtypeStruct((m, n), F32),
        grid=(m // tm,),
        in_specs=[pl.BlockSpec((tm, k), lambda i: (i, 0)),
                  pl.BlockSpec((k, n), lambda i: (0, 0)),
                  pl.BlockSpec((tm, n), lambda i: (i, 0)),
                  pl.BlockSpec((1, n), lambda i: (0, 0)),
                  pl.BlockSpec((1, n), lambda i: (0, 0))],
        out_specs=pl.BlockSpec((tm, n), lambda i: (i, 0)),
        compiler_params=pltpu.CompilerParams(
            dimension_semantics=("arbitrary",), vmem_limit_bytes=VMEM_LIMIT),
        name="out_proj_ln",
    )(a, w, x, g, b)


def _even_mix_kernel(h_ref, r_ref, wg_ref, bg_ref, gn_ref, lng_ref, lnb_ref, ws_ref, bst_ref,
                     o_ref, state_ref, wt_ref, bias_ref, *, ts):
    step = pl.program_id(0)

    @pl.when(step == 0)
    def _init():
        state_ref[...] = jnp.zeros_like(state_ref)
        r = lax.broadcasted_iota(jnp.int32, (SGU_CHUNK, SGU_CHUNK), 0)
        c = lax.broadcasted_iota(jnp.int32, (SGU_CHUNK, SGU_CHUNK), 1)
        for g in range(SGU_GROUPS):
            wt_ref[g] = jnp.where(c <= r, ws_ref[g], 0.0).astype(BF16)
            bias_ref[g] = jnp.broadcast_to(bst_ref[:, g:g + 1], (SGU_CHUNK, LANES))

    u = _dot(r_ref[0], wg_ref[...]) + bg_ref[...]
    log_a = -_softplus(-u) * (1.0 / GLA_TAU)
    r = lax.broadcasted_iota(jnp.int32, (ts, ts), 0)
    c = lax.broadcasted_iota(jnp.int32, (ts, ts), 1)
    same_chunk = (r // GLA_CHUNK) == (c // GLA_CHUNK)
    cum_mat = jnp.where(same_chunk & (c <= r), 1.0, 0.0).astype(BF16)
    la_hi, la_lo = _split_bf16(log_a)
    b_all = _dot(cum_mat, la_hi) + _dot(cum_mat, la_lo)

    rr = lax.broadcasted_iota(jnp.int32, (GLA_CHUNK, GLA_CHUNK), 0)
    cc = lax.broadcasted_iota(jnp.int32, (GLA_CHUNK, GLA_CHUNK), 1)
    causal = cc <= rr
    scale = GLA_DK ** -0.5
    gn = gn_ref[...]

    for ck in range(ts // GLA_CHUNK):
        rows = slice(ck * GLA_CHUNK, (ck + 1) * GLA_CHUNK)
        bc = b_all[rows]
        b_mid = bc[GLA_CHUNK // 2:GLA_CHUNK // 2 + 1]
        b_last = bc[GLA_CHUNK - 1:GLA_CHUNK]
        e_q = jnp.exp(bc - b_mid) * scale
        e_k = jnp.exp(b_mid - bc)
        e_kv = jnp.exp(b_last - bc)
        e_b = jnp.exp(bc) * scale
        decay = jnp.exp(b_last)
        for hd in range(GLA_HEADS):
            sl = slice(hd * GLA_DK, (hd + 1) * GLA_DK)
            qh = h_ref[_QA + hd, rows, :].astype(F32)
            kh = h_ref[_KA + hd, rows, :].astype(F32)
            vh = jnp.concatenate([h_ref[_VA + 2 * hd, rows, :], h_ref[_VA + 2 * hd + 1, rows, :]], axis=1)
            sc = _dot_nt((qh * e_q[:, sl]).astype(BF16), (kh * e_k[:, sl]).astype(BF16))
            sc = jnp.where(causal, sc, 0.0).astype(BF16)
            o = _dot(sc, vh)
            st = state_ref[hd]
            o = o + _dot_nt((qh * e_b[:, sl]).astype(BF16), st.astype(BF16))
            kk = (kh * e_kv[:, sl]).astype(BF16)
            kv_t = lax.dot_general(vh, kk, (((0,), (0,)), ((), ())), preferred_element_type=F32)
            state_ref[hd] = st * decay[:, sl] + kv_t
            ms = jnp.mean(o * o, axis=-1, keepdims=True)
            on = o * lax.rsqrt(ms + LN_EPS) * gn
            ga = jnp.concatenate([h_ref[_GA + 2 * hd, rows, :], h_ref[_GA + 2 * hd + 1, rows, :]],
                                 axis=1).astype(F32)
            o_ref[rows, hd * GLA_DV:(hd + 1) * GLA_DV] = (on * _silu(ga)).astype(o_ref.dtype)

    vg = [_gelu_tanh(h_ref[_VB + g].astype(F32)) for g in range(SGU_GROUPS)]
    tot = vg[0]
    for g in range(1, SGU_GROUPS):
        tot = tot + vg[g]
    mu = jnp.sum(tot, axis=-1, keepdims=True) * (1.0 / SGU_W)
    vc = [v - mu for v in vg]
    sq = vc[0] * vc[0]
    for g in range(1, SGU_GROUPS):
        sq = sq + vc[g] * vc[g]
    rstd = lax.rsqrt(jnp.sum(sq, axis=-1, keepdims=True) * (1.0 / SGU_W) + LN_EPS)
    for g in range(SGU_GROUPS):
        cols = slice(g * LANES, (g + 1) * LANES)
        vn = (vc[g] * rstd * lng_ref[:, cols] + lnb_ref[:, cols]).astype(BF16)
        ug = _gelu_tanh(h_ref[_UB + g].astype(F32))
        gate = _silu(h_ref[_GB + g].astype(F32))
        for ck in range(ts // SGU_CHUNK):
            rows = slice(ck * SGU_CHUNK, (ck + 1) * SGU_CHUNK)
            s = _dot(wt_ref[g], vn[rows]) + bias_ref[g]
            o_ref[rows, GLA_V + g * LANES:GLA_V + (g + 1) * LANES] = (
                ug[rows] * s * gate[rows]).astype(o_ref.dtype)


def _even_mix(h3, r3, wg, bg, gn, lng, lnb, ws, bst, ts):
    nblk, s, _ = h3.shape
    return pl.pallas_call(
        functools.partial(_even_mix_kernel, ts=ts),
        out_shape=jax.ShapeDtypeStruct((s, GLA_V + SGU_W), BF16),
        grid=(s // ts,),
        in_specs=[pl.BlockSpec((nblk, ts, LANES), lambda i: (0, i, 0)),
                  pl.BlockSpec((1, ts, LANES), lambda i: (0, i, 0)),
                  pl.BlockSpec(wg.shape, lambda i: (0, 0)),
                  pl.BlockSpec(bg.shape, lambda i: (0, 0)),
                  pl.BlockSpec(gn.shape, lambda i: (0, 0)),
                  pl.BlockSpec(lng.shape, lambda i: (0, 0)),
                  pl.BlockSpec(lnb.shape, lambda i: (0, 0)),
                  pl.BlockSpec(ws.shape, lambda i: (0, 0, 0)),
                  pl.BlockSpec(bst.shape, lambda i: (0, 0))],
        out_specs=pl.BlockSpec((ts, GLA_V + SGU_W), lambda i: (i, 0)),
        scratch_shapes=[pltpu.VMEM((GLA_HEADS, GLA_DV, GLA_DK), F32),
                        pltpu.VMEM((SGU_GROUPS, SGU_CHUNK, SGU_CHUNK), BF16),
                        pltpu.VMEM((SGU_GROUPS, SGU_CHUNK, LANES), F32)],
        compiler_params=pltpu.CompilerParams(
            dimension_semantics=("arbitrary",), vmem_limit_bytes=VMEM_LIMIT),
        name="even_mix",
    )(h3, r3, wg, bg, gn, lng, lnb, ws, bst)


def _sb_kernel(q_ref, k_ref, v_ref, g_ref, sfx_ref, o_ref, acc_ref, carry_ref, *, blk, nh):
    i = pl.program_id(1)
    r = lax.broadcasted_iota(jnp.int32, (blk, blk), 0)
    c = lax.broadcasted_iota(jnp.int32, (blk, blk), 1)
    strict = c < r

    def visit(hd, j, carry):
        start = pl.multiple_of(j * blk, blk)
        kj = k_ref[hd, pl.ds(start, blk), :]
        vj = v_ref[hd, pl.ds(start, blk), :]
        z2 = _dot_nt(q_ref[hd], kj)
        sp2 = jnp.maximum(z2, 0.0) + jnp.log(1.0 + jnp.exp2(-jnp.abs(z2))) * LOG2E
        addend = jnp.where(strict, sp2, 0.0) if carry is None else sp2
        cum = _dot(addend.astype(BF16), sfx_ref[...])
        x = z2 - cum
        if carry is not None:
            x = x - jnp.concatenate([carry] * (blk // LANES), axis=1)
        w = jnp.exp2(x)
        if carry is None:
            w = jnp.where(strict, w, 0.0)
        pv = _dot(w.astype(BF16), vj)
        return pv, jnp.broadcast_to(cum[:, 0:1], (blk, LANES))

    has_prev = i > 0
    prev = jnp.maximum(i - 1, 0)
    sd = pl.multiple_of(i * blk, blk)
    sp_ = pl.multiple_of(prev * blk, blk)
    z_d = [_dot_nt(q_ref[hd], k_ref[hd, pl.ds(sd, blk), :]) for hd in range(nh)]
    z_p = [_dot_nt(q_ref[hd], k_ref[hd, pl.ds(sp_, blk), :]) for hd in range(nh)]

    def softplus2(z2):
        return jnp.maximum(z2, 0.0) + jnp.log(1.0 + jnp.exp2(-jnp.abs(z2))) * LOG2E

    a_d = [jnp.where(strict, softplus2(z), 0.0).astype(BF16) for z in z_d]
    a_p = [softplus2(z).astype(BF16) for z in z_p]
    cum = _dot(jnp.concatenate(a_d + a_p, axis=0), sfx_ref[...])
    for hd in range(nh):
        c_d = cum[hd * blk:(hd + 1) * blk]
        c_p = cum[(nh + hd) * blk:(nh + hd + 1) * blk]
        tot_d = jnp.broadcast_to(c_d[:, 0:1], (blk, LANES))
        tot_p = jnp.broadcast_to(c_p[:, 0:1], (blk, LANES))
        w_d = jnp.where(strict, jnp.exp2(z_d[hd] - c_d), 0.0).astype(BF16)
        w_p = jnp.exp2(z_p[hd] - c_p - jnp.concatenate([tot_d] * (blk // LANES), axis=1)).astype(BF16)
        pv_d = _dot(w_d, v_ref[hd, pl.ds(sd, blk), :])
        pv_p = _dot(w_p, v_ref[hd, pl.ds(sp_, blk), :])
        acc_ref[hd] = pv_d + jnp.where(has_prev, pv_p, 0.0)
        carry_ref[hd] = tot_d + tot_p

    def more(state):
        j, low = state
        return jnp.logical_and(j >= 0, low < F32_EXP2_UNDERFLOW)

    def body(state):
        j, _ = state
        for hd in range(nh):
            pv, tot = visit(hd, j, carry_ref[hd])
            acc_ref[hd] += pv
            carry_ref[hd] += tot
        return j - 1, jnp.min(carry_ref[...])

    lax.while_loop(more, body, (i - 2, jnp.min(carry_ref[...])))
    for hd in range(nh):
        o_ref[:, hd * SB_DIM:(hd + 1) * SB_DIM] = (
            acc_ref[hd] * _silu(g_ref[hd].astype(F32))).astype(o_ref.dtype)


def _stick_breaking(h3, blk, nh):
    _, s, _ = h3.shape
    groups = SB_HEADS // nh
    rs = lax.broadcasted_iota(jnp.int32, (blk, blk), 0)
    cs = lax.broadcasted_iota(jnp.int32, (blk, blk), 1)
    sfx = jnp.where(rs >= cs, 1.0, 0.0).astype(BF16)
    return pl.pallas_call(
        functools.partial(_sb_kernel, blk=blk, nh=nh),
        out_shape=jax.ShapeDtypeStruct((s, SB_HEADS * SB_DIM), BF16),
        grid=(groups, s // blk),
        in_specs=[pl.BlockSpec((nh, blk, SB_DIM), lambda h, i: (h, i, 0)),
                  pl.BlockSpec((nh, s, SB_DIM), lambda h, i: (groups + h, 0, 0), pipeline_mode=pl.Buffered(1)),
                  pl.BlockSpec((nh, s, SB_DIM), lambda h, i: (2 * groups + h, 0, 0), pipeline_mode=pl.Buffered(1)),
                  pl.BlockSpec((nh, blk, SB_DIM), lambda h, i: (3 * groups + h, i, 0)),
                  pl.BlockSpec(sfx.shape, lambda h, i: (0, 0))],
        out_specs=pl.BlockSpec((blk, nh * SB_DIM), lambda h, i: (i, h)),
        scratch_shapes=[pltpu.VMEM((nh, blk, SB_DIM), F32), pltpu.VMEM((nh, blk, LANES), F32)],
        compiler_params=pltpu.CompilerParams(
            dimension_semantics=("arbitrary", "arbitrary"), vmem_limit_bytes=VMEM_LIMIT),
        name="stick_breaking",
    )(h3, h3, h3, h3, sfx)


def _pick(n, candidates):
    for c in candidates:
        if n % c == 0:
            return c
    raise ValueError(f"no tile of {candidates} divides {n}")


def kernel(x, even_w_in, even_gla_w_gate2, even_gla_b_gate, even_gla_norm_g, even_sgu_ln_g, even_sgu_ln_b,
           even_sgu_w_s, even_sgu_b_s, even_w_out, odd_w_in, odd_w_out, post_ln_g, post_ln_b):
    bsz, seq, d = x.shape
    assert bsz == 1 and d == D_MODEL
    xs = x[0]
    tm = _pick(seq, (1024, 512, 256))
    tl = _pick(seq, (256,))
    to = _pick(seq, (1024, 512, 256))

    wt = jnp.swapaxes(even_w_in[0], 0, 1).astype(BF16)
    o_r = 2 * GLA_QK + GLA_V
    assert o_r % EVEN_TN == 0
    w_rank = jnp.concatenate([wt[o_r:o_r + GLA_RANK], jnp.zeros((LANES - GLA_RANK, d), BF16)], axis=0)
    wg = jnp.concatenate([even_gla_w_gate2[0], jnp.zeros((LANES - GLA_RANK, GLA_QK), F32)], axis=0).astype(BF16)
    h3, r3 = _project(xs, wt, jnp.ones((1, EVEN_BLOCKS * LANES), F32), tm, EVEN_TN, True,
                      row_start=lambda j: j * EVEN_TN + jnp.where(j * EVEN_TN >= o_r, GLA_RANK, 0),
                      side_w=w_rank)
    mix = _even_mix(h3, r3, wg, even_gla_b_gate[0][None, :], even_gla_norm_g[0][None, :],
                    even_sgu_ln_g[0][None, :], even_sgu_ln_b[0][None, :],
                    even_sgu_w_s[0], even_sgu_b_s[0].T, tl)
    x1 = _out_proj_ln(mix, even_w_out[0].astype(BF16), xs, post_ln_g[0][None, :], post_ln_b[0][None, :], to, OUT_SUB)

    q_scale = jnp.concatenate([jnp.full((1, SB_HEADS * SB_DIM), SB_DIM ** -0.5 * LOG2E, F32),
                               jnp.ones((1, 3 * SB_HEADS * SB_DIM), F32)], axis=1)
    h3, = _project(x1, odd_w_in[0].astype(BF16), q_scale, tm, 1024, False)
    att = _stick_breaking(h3, 256, 4)
    x2 = _out_proj_ln(att, odd_w_out[0].astype(BF16), x1, post_ln_g[1][None, :], post_ln_b[1][None, :], to, OUT_SUB)
    return x2[None]
```

```python
import functools

import jax
import jax.numpy as jnp
from jax import lax
from jax.experimental import pallas as pl
from jax.experimental.pallas import tpu as pltpu

F32 = jnp.float32
BF16 = jnp.bfloat16

LANES = 128
BF16_SUBLANES = 16
D_MODEL = 2048
DEPTH = 2
GLA_HEADS = 4
GLA_DK = 128
GLA_DV = 256
GLA_QK = GLA_HEADS * GLA_DK
GLA_V = GLA_HEADS * GLA_DV
GLA_RANK = 16
GLA_TAU = 16.0
GLA_CHUNK = 64
SGU_W = 1024
SGU_CHUNK = 128
SGU_GROUPS = SGU_W // LANES
SB_HEADS = 16
SB_DIM = 128
ALPHA = (2.0 * DEPTH) ** 0.25
LN_EPS = 1e-5

_QA, _KA, _VA, _GA, _UB, _VB, _GB = 0, 4, 8, 16, 24, 32, 40
EVEN_BLOCKS = 48
OUT_TN = 256
OUT_SUB = 128
EVEN_TN = 1024

VMEM_LIMIT = 56 * 1024 * 1024
LOG2E = 1.4426950408889634
F32_EXP2_UNDERFLOW = 151.0


def _softplus(x):
    return jnp.maximum(x, 0.0) + jnp.log(1.0 + jnp.exp(-jnp.abs(x)))


def _silu(x):
    return x / (1.0 + jnp.exp2(x * -LOG2E))


def _gelu_tanh(x):
    k1 = -2.0 * 0.7978845608028654 * LOG2E
    return x / (1.0 + jnp.exp2(x * (k1 + (k1 * 0.044715) * (x * x))))


def _split_bf16(x):
    hi = x.astype(BF16)
    lo = (x - hi.astype(F32)).astype(BF16)
    return hi, lo


def _dot(a, b):
    return jnp.dot(a, b, preferred_element_type=F32)


def _dot_nt(a, b):
    return lax.dot_general(a, b, (((1,), (1,)), ((), ())), preferred_element_type=F32)


def _proj_kernel(*refs, w_transposed, has_side):
    if has_side:
        a_ref, w_ref, cs_ref, ws_ref, o_ref, os_ref, ab_ref = refs
    else:
        a_ref, w_ref, cs_ref, o_ref, ab_ref = refs

    @pl.when(pl.program_id(1) == 0)
    def _first():
        ab_ref[...] = a_ref[...].astype(BF16)
        if has_side:
            os_ref[0] = _dot_nt(ab_ref[...], ws_ref[...]).astype(os_ref.dtype)

    dot = _dot_nt if w_transposed else _dot
    r = dot(ab_ref[...], w_ref[...]) * cs_ref[...]
    for c in range(o_ref.shape[0]):
        o_ref[c] = r[:, c * LANES:(c + 1) * LANES].astype(o_ref.dtype)


def _project(a, w, col_scale, tm, tn, w_transposed, row_start=None, side_w=None):
    m, k = a.shape
    n = col_scale.shape[1]
    nb = tn // LANES
    if not w_transposed:
        w_spec = pl.BlockSpec((k, tn), lambda i, j: (0, j))
    elif row_start is None:
        w_spec = pl.BlockSpec((tn, k), lambda i, j: (j, 0))
    else:
        w_spec = pl.BlockSpec((pl.Element(tn), pl.Element(k)),
                              lambda i, j: (pl.multiple_of(row_start(j), BF16_SUBLANES), 0))
    in_specs = [pl.BlockSpec((tm, k), lambda i, j: (i, 0)), w_spec, pl.BlockSpec((1, tn), lambda i, j: (0, j))]
    out_shape = [jax.ShapeDtypeStruct((n // LANES, m, LANES), BF16)]
    out_specs = [pl.BlockSpec((nb, tm, LANES), lambda i, j: (j, i, 0))]
    operands = [a, w, col_scale]
    if side_w is not None:
        in_specs.append(pl.BlockSpec(side_w.shape, lambda i, j: (0, 0)))
        out_shape.append(jax.ShapeDtypeStruct((1, m, LANES), BF16))
        out_specs.append(pl.BlockSpec((1, tm, LANES), lambda i, j: (0, i, 0)))
        operands.append(side_w)
    return pl.pallas_call(
        functools.partial(_proj_kernel, w_transposed=w_transposed, has_side=side_w is not None),
        out_shape=out_shape,
        grid=(m // tm, n // tn),
        in_specs=in_specs,
        out_specs=out_specs,
        scratch_shapes=[pltpu.VMEM((tm, k), BF16)],
        compiler_params=pltpu.CompilerParams(
            dimension_semantics=("arbitrary", "arbitrary"), vmem_limit_bytes=VMEM_LIMIT),
        name="in_proj",
    )(*operands)


def _out_ln_tile(a_ref, w_ref, x_ref, g_ref, b_ref, o_ref, sub):
    for t in range(o_ref.shape[0] // sub):
        rows = slice(t * sub, (t + 1) * sub)
        r = ALPHA * x_ref[rows, :] + _dot(a_ref[rows, :], w_ref[...])
        mu = jnp.mean(r, axis=-1, keepdims=True)
        rc = r - mu
        var = jnp.mean(rc * rc, axis=-1, keepdims=True)
        o_ref[rows, :] = rc * lax.rsqrt(var + LN_EPS) * g_ref[...] + b_ref[...]


def _out_proj_ln(a, w, x, g, b, tm, sub):
    m, k = a.shape
    n = w.shape[1]
    return pl.pallas_call(
        functools.partial(_out_ln_tile, sub=sub),
        out_shape=jax.ShapeDtypeStruct((m, n), F32),
        grid=(m // tm,),
        in_specs=[pl.BlockSpec((tm, k), lambda i: (i, 0)),
                  pl.BlockSpec((k, n), lambda i: (0, 0)),
                  pl.BlockSpec((tm, n), lambda i: (i, 0)),
                  pl.BlockSpec((1, n), lambda i: (0, 0)),
                  pl.BlockSpec((1, n), lambda i: (0, 0))],
        out_specs=pl.BlockSpec((tm, n), lambda i: (i, 0)),
        compiler_params=pltpu.CompilerParams(
            dimension_semantics=("arbitrary",), vmem_limit_bytes=VMEM_LIMIT),
        name="out_proj_ln",
    )(a, w, x, g, b)


def _mix_pieces(h_ref, r_ref, wg_ref, bg_ref, gn_ref, lng_ref, lnb_ref, o_ref, state_ref, wt_ref, bias_ref, ts):
    u = _dot(r_ref[0], wg_ref[...]) + bg_ref[...]
    log_a = -_softplus(-u) * (1.0 / GLA_TAU)
    r = lax.broadcasted_iota(jnp.int32, (ts, ts), 0)
    c = lax.broadcasted_iota(jnp.int32, (ts, ts), 1)
    same_chunk = (r // GLA_CHUNK) == (c // GLA_CHUNK)
    cum_mat = jnp.where(same_chunk & (c <= r), 1.0, 0.0).astype(BF16)
    la_hi, la_lo = _split_bf16(log_a)
    b_all = _dot(cum_mat, la_hi) + _dot(cum_mat, la_lo)

    rr = lax.broadcasted_iota(jnp.int32, (GLA_CHUNK, GLA_CHUNK), 0)
    cc = lax.broadcasted_iota(jnp.int32, (GLA_CHUNK, GLA_CHUNK), 1)
    causal = cc <= rr
    scale = GLA_DK ** -0.5
    gn = gn_ref[...]

    for ck in range(ts // GLA_CHUNK):
        rows = slice(ck * GLA_CHUNK, (ck + 1) * GLA_CHUNK)
        bc = b_all[rows]
        b_mid = bc[GLA_CHUNK // 2:GLA_CHUNK // 2 + 1]
        b_last = bc[GLA_CHUNK - 1:GLA_CHUNK]
        e_q = jnp.exp(bc - b_mid) * scale
        e_k = jnp.exp(b_mid - bc)
        e_kv = jnp.exp(b_last - bc)
        e_b = jnp.exp(bc) * scale
        decay = jnp.exp(b_last)
        for hd in range(GLA_HEADS):
            sl = slice(hd * GLA_DK, (hd + 1) * GLA_DK)
            qh = h_ref[_QA + hd, rows, :].astype(F32)
            kh = h_ref[_KA + hd, rows, :].astype(F32)
            vh = jnp.concatenate([h_ref[_VA + 2 * hd, rows, :], h_ref[_VA + 2 * hd + 1, rows, :]], axis=1)
            sc = _dot_nt((qh * e_q[:, sl]).astype(BF16), (kh * e_k[:, sl]).astype(BF16))
            sc = jnp.where(causal, sc, 0.0).astype(BF16)
            o = _dot(sc, vh)
            st = state_ref[hd]
            o = o + _dot_nt((qh * e_b[:, sl]).astype(BF16), st.astype(BF16))
            kk = (kh * e_kv[:, sl]).astype(BF16)
            kv_t = lax.dot_general(vh, kk, (((0,), (0,)), ((), ())), preferred_element_type=F32)
            state_ref[hd] = st * decay[:, sl] + kv_t
            ms = jnp.mean(o * o, axis=-1, keepdims=True)
            on = o * lax.rsqrt(ms + LN_EPS) * gn
            ga = jnp.concatenate([h_ref[_GA + 2 * hd, rows, :], h_ref[_GA + 2 * hd + 1, rows, :]],
                                 axis=1).astype(F32)
            o_ref[rows, hd * GLA_DV:(hd + 1) * GLA_DV] = (on * _silu(ga)).astype(o_ref.dtype)
            yield

    vg = [_gelu_tanh(h_ref[_VB + g].astype(F32)) for g in range(SGU_GROUPS)]
    tot = vg[0]
    for g in range(1, SGU_GROUPS):
        tot = tot + vg[g]
    mu = jnp.sum(tot, axis=-1, keepdims=True) * (1.0 / SGU_W)
    vc = [v - mu for v in vg]
    sq = vc[0] * vc[0]
    for g in range(1, SGU_GROUPS):
        sq = sq + vc[g] * vc[g]
    rstd = lax.rsqrt(jnp.sum(sq, axis=-1, keepdims=True) * (1.0 / SGU_W) + LN_EPS)
    for g in range(SGU_GROUPS):
        cols = slice(g * LANES, (g + 1) * LANES)
        vn = (vc[g] * rstd * lng_ref[:, cols] + lnb_ref[:, cols]).astype(BF16)
        ug = _gelu_tanh(h_ref[_UB + g].astype(F32))
        gate = _silu(h_ref[_GB + g].astype(F32))
        for ck in range(ts // SGU_CHUNK):
            rows = slice(ck * SGU_CHUNK, (ck + 1) * SGU_CHUNK)
            s = _dot(wt_ref[g], vn[rows]) + bias_ref[g]
            o_ref[rows, GLA_V + g * LANES:GLA_V + (g + 1) * LANES] = (
                ug[rows] * s * gate[rows]).astype(o_ref.dtype)
        yield


def _out_ln_pieces(a_ref, w_ref, x_ref, g_ref, b_ref, o_ref, y_ref, sub, tn):
    n = w_ref.shape[1]
    for t in range(o_ref.shape[0] // sub):
        rows = slice(t * sub, (t + 1) * sub)
        for c in range(n // tn):
            cols = slice(c * tn, (c + 1) * tn)
            y_ref[rows, cols] = _dot(a_ref[rows, :], w_ref[:, cols])
            yield
        r = ALPHA * x_ref[rows, :] + y_ref[rows, :]
        mu = jnp.mean(r, axis=-1, keepdims=True)
        rc = r - mu
        var = jnp.mean(rc * rc, axis=-1, keepdims=True)
        o_ref[rows, :] = rc * lax.rsqrt(var + LN_EPS) * g_ref[...] + b_ref[...]
        yield


def _even_layer_kernel(h_ref, r_ref, wg_ref, bg_ref, gn_ref, lng_ref, lnb_ref, ws_ref, bst_ref,
                       wo_ref, x_ref, pg_ref, pb_ref, o_ref, state_ref, wt_ref, bias_ref, mix_ref, y_ref,
                       *, ts, steps, sub):
    t = pl.program_id(0)

    @pl.when(t == 0)
    def _init():
        state_ref[...] = jnp.zeros_like(state_ref)
        mix_ref[1] = jnp.zeros(mix_ref.shape[1:], mix_ref.dtype)
        r = lax.broadcasted_iota(jnp.int32, (SGU_CHUNK, SGU_CHUNK), 0)
        c = lax.broadcasted_iota(jnp.int32, (SGU_CHUNK, SGU_CHUNK), 1)
        for g in range(SGU_GROUPS):
            wt_ref[g] = jnp.where(c <= r, ws_ref[g], 0.0).astype(BF16)
            bias_ref[g] = jnp.broadcast_to(bst_ref[:, g:g + 1], (SGU_CHUNK, LANES))

    def projection(slot):
        return _out_ln_pieces(mix_ref.at[slot], wo_ref, x_ref, pg_ref, pb_ref, o_ref, y_ref, sub, OUT_TN)

    def project(slot):
        for _ in projection(slot):
            pass

    def both(slot):
        proj = projection(1 - slot)
        for _ in _mix_pieces(h_ref, r_ref, wg_ref, bg_ref, gn_ref, lng_ref, lnb_ref, mix_ref.at[slot],
                             state_ref, wt_ref, bias_ref, ts):
            next(proj, None)
        for _ in proj:
            pass

    pl.when(jnp.logical_and(t < steps, t % 2 == 0))(functools.partial(both, 0))
    pl.when(jnp.logical_and(t < steps, t % 2 == 1))(functools.partial(both, 1))
    pl.when(t == steps)(functools.partial(project, (steps - 1) % 2))


def _even_layer(h3, r3, wg, bg, gn, lng, lnb, ws, bst, wo, x, pg, pb, ts, sub):
    nblk, s, _ = h3.shape
    steps = s // ts
    d = wo.shape[1]
    cur = lambda t: (0, jnp.minimum(t, steps - 1), 0)
    prev = lambda t: (jnp.maximum(t - 1, 0), 0)
    const2 = lambda t: (0, 0)
    return pl.pallas_call(
        functools.partial(_even_layer_kernel, ts=ts, steps=steps, sub=sub),
        out_shape=jax.ShapeDtypeStruct((s, d), F32),
        grid=(steps + 1,),
        in_specs=[pl.BlockSpec((nblk, ts, LANES), cur),
                  pl.BlockSpec((1, ts, LANES), cur),
                  pl.BlockSpec(wg.shape, const2),
                  pl.BlockSpec(bg.shape, const2),
                  pl.BlockSpec(gn.shape, const2),
                  pl.BlockSpec(lng.shape, const2),
                  pl.BlockSpec(lnb.shape, const2),
                  pl.BlockSpec(ws.shape, lambda t: (0, 0, 0)),
                  pl.BlockSpec(bst.shape, const2),
                  pl.BlockSpec(wo.shape, const2),
                  pl.BlockSpec((ts, d), prev),
                  pl.BlockSpec(pg.shape, const2),
                  pl.BlockSpec(pb.shape, const2)],
        out_specs=pl.BlockSpec((ts, d), prev),
        scratch_shapes=[pltpu.VMEM((GLA_HEADS, GLA_DV, GLA_DK), F32),
                        pltpu.VMEM((SGU_GROUPS, SGU_CHUNK, SGU_CHUNK), BF16),
                        pltpu.VMEM((SGU_GROUPS, SGU_CHUNK, LANES), F32),
                        pltpu.VMEM((2, ts, GLA_V + SGU_W), BF16),
                        pltpu.VMEM((ts, d), F32)],
        compiler_params=pltpu.CompilerParams(
            dimension_semantics=("arbitrary",), vmem_limit_bytes=VMEM_LIMIT),
        name="even_layer",
    )(h3, r3, wg, bg, gn, lng, lnb, ws, bst, wo, x, pg, pb)


def _sb_kernel(q_ref, k_ref, v_ref, g_ref, sfx_ref, o_ref, acc_ref, carry_ref, *, blk, nh):
    i = pl.program_id(1)
    r = lax.broadcasted_iota(jnp.int32, (blk, blk), 0)
    c = lax.broadcasted_iota(jnp.int32, (blk, blk), 1)
    strict = c < r

    def visit(hd, j, carry):
        start = pl.multiple_of(j * blk, blk)
        kj = k_ref[hd, pl.ds(start, blk), :]
        vj = v_ref[hd, pl.ds(start, blk), :]
        z2 = _dot_nt(q_ref[hd], kj)
        sp2 = jnp.maximum(z2, 0.0) + jnp.log(1.0 + jnp.exp2(-jnp.abs(z2))) * LOG2E
        addend = jnp.where(strict, sp2, 0.0) if carry is None else sp2
        cum = _dot(addend.astype(BF16), sfx_ref[...])
        x = z2 - cum
        if carry is not None:
            x = x - jnp.concatenate([carry] * (blk // LANES), axis=1)
        w = jnp.exp2(x)
        if carry is None:
            w = jnp.where(strict, w, 0.0)
        pv = _dot(w.astype(BF16), vj)
        return pv, jnp.broadcast_to(cum[:, 0:1], (blk, LANES))

    has_prev = i > 0
    prev = jnp.maximum(i - 1, 0)
    sd = pl.multiple_of(i * blk, blk)
    sp_ = pl.multiple_of(prev * blk, blk)
    z_d = [_dot_nt(q_ref[hd], k_ref[hd, pl.ds(sd, blk), :]) for hd in range(nh)]
    z_p = [_dot_nt(q_ref[hd], k_ref[hd, pl.ds(sp_, blk), :]) for hd in range(nh)]

    def softplus2(z2):
        return jnp.maximum(z2, 0.0) + jnp.log(1.0 + jnp.exp2(-jnp.abs(z2))) * LOG2E

    a_d = [jnp.where(strict, softplus2(z), 0.0).astype(BF16) for z in z_d]
    a_p = [softplus2(z).astype(BF16) for z in z_p]
    cum = _dot(jnp.concatenate(a_d + a_p, axis=0), sfx_ref[...])
    for hd in range(nh):
        c_d = cum[hd * blk:(hd + 1) * blk]
        c_p = cum[(nh + hd) * blk:(nh + hd + 1) * blk]
        tot_d = jnp.broadcast_to(c_d[:, 0:1], (blk, LANES))
        tot_p = jnp.broadcast_to(c_p[:, 0:1], (blk, LANES))
        w_d = jnp.where(strict, jnp.exp2(z_d[hd] - c_d), 0.0).astype(BF16)
        w_p = jnp.exp2(z_p[hd] - c_p - jnp.concatenate([tot_d] * (blk // LANES), axis=1)).astype(BF16)
        pv_d = _dot(w_d, v_ref[hd, pl.ds(sd, blk), :])
        pv_p = _dot(w_p, v_ref[hd, pl.ds(sp_, blk), :])
        acc_ref[hd] = pv_d + jnp.where(has_prev, pv_p, 0.0)
        carry_ref[hd] = tot_d + tot_p

    def more(state):
        j, low = state
        return jnp.logical_and(j >= 0, low < F32_EXP2_UNDERFLOW)

    def body(state):
        j, _ = state
        for hd in range(nh):
            pv, tot = visit(hd, j, carry_ref[hd])
            acc_ref[hd] += pv
            carry_ref[hd] += tot
        return j - 1, jnp.min(carry_ref[...])

    lax.while_loop(more, body, (i - 2, jnp.min(carry_ref[...])))
    for hd in range(nh):
        o_ref[:, hd * SB_DIM:(hd + 1) * SB_DIM] = (
            acc_ref[hd] * _silu(g_ref[hd].astype(F32))).astype(o_ref.dtype)


def _stick_breaking(h3, blk, nh):
    _, s, _ = h3.shape
    groups = SB_HEADS // nh
    rs = lax.broadcasted_iota(jnp.int32, (blk, blk), 0)
    cs = lax.broadcasted_iota(jnp.int32, (blk, blk), 1)
    sfx = jnp.where(rs >= cs, 1.0, 0.0).astype(BF16)
    return pl.pallas_call(
        functools.partial(_sb_kernel, blk=blk, nh=nh),
        out_shape=jax.ShapeDtypeStruct((s, SB_HEADS * SB_DIM), BF16),
        grid=(groups, s // blk),
        in_specs=[pl.BlockSpec((nh, blk, SB_DIM), lambda h, i: (h, i, 0)),
                  pl.BlockSpec((nh, s, SB_DIM), lambda h, i: (groups + h, 0, 0), pipeline_mode=pl.Buffered(1)),
                  pl.BlockSpec((nh, s, SB_DIM), lambda h, i: (2 * groups + h, 0, 0), pipeline_mode=pl.Buffered(1)),
                  pl.BlockSpec((nh, blk, SB_DIM), lambda h, i: (3 * groups + h, i, 0)),
                  pl.BlockSpec(sfx.shape, lambda h, i: (0, 0))],
        out_specs=pl.BlockSpec((blk, nh * SB_DIM), lambda h, i: (i, h)),
        scratch_shapes=[pltpu.VMEM((nh, blk, SB_DIM), F32), pltpu.VMEM((nh, blk, LANES), F32)],
        compiler_params=pltpu.CompilerParams(
            dimension_semantics=("arbitrary", "arbitrary"), vmem_limit_bytes=VMEM_LIMIT),
        name="stick_breaking",
    )(h3, h3, h3, h3, sfx)


def _pick(n, candidates):
    for c in candidates:
        if n % c == 0:
            return c
    raise ValueError(f"no tile of {candidates} divides {n}")


def kernel(x, even_w_in, even_gla_w_gate2, even_gla_b_gate, even_gla_norm_g, even_sgu_ln_g, even_sgu_ln_b,
           even_sgu_w_s, even_sgu_b_s, even_w_out, odd_w_in, odd_w_out, post_ln_g, post_ln_b):
    bsz, seq, d = x.shape
    assert bsz == 1 and d == D_MODEL
    xs = x[0]
    tm = _pick(seq, (1024, 512, 256))
    tl = _pick(seq, (256,))
    to = _pick(seq, (1024, 512, 256))

    wt = jnp.swapaxes(even_w_in[0], 0, 1).astype(BF16)
    o_r = 2 * GLA_QK + GLA_V
    assert o_r % EVEN_TN == 0
    w_rank = jnp.concatenate([wt[o_r:o_r + GLA_RANK], jnp.zeros((LANES - GLA_RANK, d), BF16)], axis=0)
    wg = jnp.concatenate([even_gla_w_gate2[0], jnp.zeros((LANES - GLA_RANK, GLA_QK), F32)], axis=0).astype(BF16)
    h3, r3 = _project(xs, wt, jnp.ones((1, EVEN_BLOCKS * LANES), F32), tm, EVEN_TN, True,
                      row_start=lambda j: j * EVEN_TN + jnp.where(j * EVEN_TN >= o_r, GLA_RANK, 0),
                      side_w=w_rank)
    x1 = _even_layer(h3, r3, wg, even_gla_b_gate[0][None, :], even_gla_norm_g[0][None, :],
                     even_sgu_ln_g[0][None, :], even_sgu_ln_b[0][None, :],
                     even_sgu_w_s[0], even_sgu_b_s[0].T, even_w_out[0].astype(BF16), xs,
                     post_ln_g[0][None, :], post_ln_b[0][None, :], tl, OUT_SUB)

    q_scale = jnp.concatenate([jnp.full((1, SB_HEADS * SB_DIM), SB_DIM ** -0.5 * LOG2E, F32),
                               jnp.ones((1, 3 * SB_HEADS * SB_DIM), F32)], axis=1)
    h3, = _project(x1, odd_w_in[0].astype(BF16), q_scale, tm, 1024, False)
    att = _stick_breaking(h3, 256, 4)
    x2 = _out_proj_ln(att, odd_w_out[0].astype(BF16), x1, post_ln_g[1][None, :], post_ln_b[1][None, :], to, OUT_SUB)
    return x2[None]
```

```python
import functools

import jax
import jax.numpy as jnp
from jax import lax
from jax.experimental import pallas as pl
from jax.experimental.pallas import tpu as pltpu

F32 = jnp.float32
BF16 = jnp.bfloat16

LANES = 128
BF16_SUBLANES = 16
D_MODEL = 2048
DEPTH = 2
GLA_HEADS = 4
GLA_DK = 128
GLA_DV = 256
GLA_QK = GLA_HEADS * GLA_DK
GLA_V = GLA_HEADS * GLA_DV
GLA_RANK = 16
GLA_TAU = 16.0
GLA_CHUNK = 64
SGU_W = 1024
SGU_CHUNK = 128
SGU_GROUPS = SGU_W // LANES
SB_HEADS = 16
SB_DIM = 128
ALPHA = (2.0 * DEPTH) ** 0.25
LN_EPS = 1e-5

_QA, _KA, _VA, _GA, _UB, _VB, _GB = 0, 4, 8, 16, 24, 32, 40
EVEN_BLOCKS = 48
OUT_TN = 256
OUT_SUB = 128
EVEN_TN = 1024

VMEM_LIMIT = 56 * 1024 * 1024
LOG2E = 1.4426950408889634
F32_EXP2_UNDERFLOW = 151.0


def _softplus(x):
    return jnp.maximum(x, 0.0) + jnp.log(1.0 + jnp.exp(-jnp.abs(x)))


def _silu(x):
    return x / (1.0 + jnp.exp2(x * -LOG2E))


def _gelu_tanh(x):
    k1 = -2.0 * 0.7978845608028654 * LOG2E
    return x / (1.0 + jnp.exp2(x * (k1 + (k1 * 0.044715) * (x * x))))


def _split_bf16(x):
    hi = x.astype(BF16)
    lo = (x - hi.astype(F32)).astype(BF16)
    return hi, lo


def _dot(a, b):
    return jnp.dot(a, b, preferred_element_type=F32)


def _dot_nt(a, b):
    return lax.dot_general(a, b, (((1,), (1,)), ((), ())), preferred_element_type=F32)


def _proj_kernel(*refs, w_transposed, has_side):
    if has_side:
        a_ref, w_ref, cs_ref, ws_ref, o_ref, os_ref, ab_ref = refs
    else:
        a_ref, w_ref, cs_ref, o_ref, ab_ref = refs

    @pl.when(pl.program_id(1) == 0)
    def _first():
        ab_ref[...] = a_ref[...].astype(BF16)
        if has_side:
            os_ref[0] = _dot_nt(ab_ref[...], ws_ref[...]).astype(os_ref.dtype)

    dot = _dot_nt if w_transposed else _dot
    r = dot(ab_ref[...], w_ref[...]) * cs_ref[...]
    for c in range(o_ref.shape[0]):
        o_ref[c] = r[:, c * LANES:(c + 1) * LANES].astype(o_ref.dtype)


def _project(a, w, col_scale, tm, tn, w_transposed, row_start=None, side_w=None):
    m, k = a.shape
    n = col_scale.shape[1]
    nb = tn // LANES
    if not w_transposed:
        w_spec = pl.BlockSpec((k, tn), lambda i, j: (0, j))
    elif row_start is None:
        w_spec = pl.BlockSpec((tn, k), lambda i, j: (j, 0))
    else:
        w_spec = pl.BlockSpec((pl.Element(tn), pl.Element(k)),
                              lambda i, j: (pl.multiple_of(row_start(j), BF16_SUBLANES), 0))
    in_specs = [pl.BlockSpec((tm, k), lambda i, j: (i, 0)), w_spec, pl.BlockSpec((1, tn), lambda i, j: (0, j))]
    out_shape = [jax.ShapeDtypeStruct((n // LANES, m, LANES), BF16)]
    out_specs = [pl.BlockSpec((nb, tm, LANES), lambda i, j: (j, i, 0))]
    operands = [a, w, col_scale]
    if side_w is not None:
        in_specs.append(pl.BlockSpec(side_w.shape, lambda i, j: (0, 0)))
        out_shape.append(jax.ShapeDtypeStruct((1, m, LANES), BF16))
        out_specs.append(pl.BlockSpec((1, tm, LANES), lambda i, j: (0, i, 0)))
        operands.append(side_w)
    return pl.pallas_call(
        functools.partial(_proj_kernel, w_transposed=w_transposed, has_side=side_w is not None),
        out_shape=out_shape,
        grid=(m // tm, n // tn),
        in_specs=in_specs,
        out_specs=out_specs,
        scratch_shapes=[pltpu.VMEM((tm, k), BF16)],
        compiler_params=pltpu.CompilerParams(
            dimension_semantics=("arbitrary", "arbitrary"), vmem_limit_bytes=VMEM_LIMIT),
        name="in_proj",
    )(*operands)


def _out_ln_tile(a_ref, w_ref, x_ref, g_ref, b_ref, o_ref, sub):
    for t in range(o_ref.shape[0] // sub):
        rows = slice(t * sub, (t + 1) * sub)
        r = ALPHA * x_ref[rows, :] + _dot(a_ref[rows, :], w_ref[...])
        mu = jnp.mean(r, axis=-1, keepdims=True)
        rc = r - mu
        var = jnp.mean(rc * rc, axis=-1, keepdims=True)
        o_ref[rows, :] = rc * lax.rsqrt(var + LN_EPS) * g_ref[...] + b_ref[...]


def _out_proj_ln(a, w, x, g, b, tm, sub):
    m, k = a.shape
    n = w.shape[1]
    return pl.pallas_call(
        functools.partial(_out_ln_tile, sub=sub),
        out_shape=jax.ShapeDtypeStruct((m, n), F32),
        grid=(m // tm,),
        in_specs=[pl.BlockSpec((tm, k), lambda i: (i, 0)),
                  pl.BlockSpec((k, n), lambda i: (0, 0)),
                  pl.BlockSpec((tm, n), lambda i: (i, 0)),
                  pl.BlockSpec((1, n), lambda i: (0, 0)),
                  pl.BlockSpec((1, n), lambda i: (0, 0))],
        out_specs=pl.BlockSpec((tm, n), lambda i: (i, 0)),
        compiler_params=pltpu.CompilerParams(
            dimension_semantics=("arbitrary",), vmem_limit_bytes=VMEM_LIMIT),
        name="out_proj_ln",
    )(a, w, x, g, b)


def _mix_pieces(h_ref, r_ref, wg_ref, bg_ref, gn_ref, lng_ref, lnb_ref, o_ref, state_ref, wt_ref, bias_ref, ts):
    u = _dot(r_ref[0], wg_ref[...]) + bg_ref[...]
    log_a = -_softplus(-u) * (1.0 / GLA_TAU)
    r = lax.broadcasted_iota(jnp.int32, (ts, ts), 0)
    c = lax.broadcasted_iota(jnp.int32, (ts, ts), 1)
    same_chunk = (r // GLA_CHUNK) == (c // GLA_CHUNK)
    cum_mat = jnp.where(same_chunk & (c <= r), 1.0, 0.0).astype(BF16)
    la_hi, la_lo = _split_bf16(log_a)
    b_all = _dot(cum_mat, la_hi) + _dot(cum_mat, la_lo)

    rr = lax.broadcasted_iota(jnp.int32, (GLA_CHUNK, GLA_CHUNK), 0)
    cc = lax.broadcasted_iota(jnp.int32, (GLA_CHUNK, GLA_CHUNK), 1)
    causal = cc <= rr
    scale = GLA_DK ** -0.5
    gn = gn_ref[...]

    for ck in range(ts // GLA_CHUNK):
        rows = slice(ck * GLA_CHUNK, (ck + 1) * GLA_CHUNK)
        bc = b_all[rows]
        b_mid = bc[GLA_CHUNK // 2:GLA_CHUNK // 2 + 1]
        b_last = bc[GLA_CHUNK - 1:GLA_CHUNK]
        e_q = jnp.exp(bc - b_mid) * scale
        e_k = jnp.exp(b_mid - bc)
        e_kv = jnp.exp(b_last - bc)
        e_b = jnp.exp(bc) * scale
        decay = jnp.exp(b_last)
        for hd in range(GLA_HEADS):
            sl = slice(hd * GLA_DK, (hd + 1) * GLA_DK)
            qh = h_ref[_QA + hd, rows, :].astype(F32)
            kh = h_ref[_KA + hd, rows, :].astype(F32)
            vh = jnp.concatenate([h_ref[_VA + 2 * hd, rows, :], h_ref[_VA + 2 * hd + 1, rows, :]], axis=1)
            sc = _dot_nt((qh * e_q[:, sl]).astype(BF16), (kh * e_k[:, sl]).astype(BF16))
            sc = jnp.where(causal, sc, 0.0).astype(BF16)
            o = _dot(sc, vh)
            st = state_ref[hd]
            o = o + _dot_nt((qh * e_b[:, sl]).astype(BF16), st.astype(BF16))
            kk = (kh * e_kv[:, sl]).astype(BF16)
            kv_t = lax.dot_general(vh, kk, (((0,), (0,)), ((), ())), preferred_element_type=F32)
            state_ref[hd] = st * decay[:, sl] + kv_t
            ms = jnp.mean(o * o, axis=-1, keepdims=True)
            on = o * lax.rsqrt(ms + LN_EPS) * gn
            ga = jnp.concatenate([h_ref[_GA + 2 * hd, rows, :], h_ref[_GA + 2 * hd + 1, rows, :]],
                                 axis=1).astype(F32)
            o_ref[rows, hd * GLA_DV:(hd + 1) * GLA_DV] = (on * _silu(ga)).astype(o_ref.dtype)
            yield

    vg = [_gelu_tanh(h_ref[_VB + g].astype(F32)) for g in range(SGU_GROUPS)]
    tot = vg[0]
    for g in range(1, SGU_GROUPS):
        tot = tot + vg[g]
    mu = jnp.sum(tot, axis=-1, keepdims=True) * (1.0 / SGU_W)
    vc = [v - mu for v in vg]
    sq = vc[0] * vc[0]
    for g in range(1, SGU_GROUPS):
        sq = sq + vc[g] * vc[g]
    rstd = lax.rsqrt(jnp.sum(sq, axis=-1, keepdims=True) * (1.0 / SGU_W) + LN_EPS)
    for g in range(SGU_GROUPS):
        cols = slice(g * LANES, (g + 1) * LANES)
        vn = (vc[g] * rstd * lng_ref[:, cols] + lnb_ref[:, cols]).astype(BF16)
        ug = _gelu_tanh(h_ref[_UB + g].astype(F32))
        gate = _silu(h_ref[_GB + g].astype(F32))
        for ck in range(ts // SGU_CHUNK):
            rows = slice(ck * SGU_CHUNK, (ck + 1) * SGU_CHUNK)
            s = _dot(wt_ref[g], vn[rows]) + bias_ref[g]
            o_ref[rows, GLA_V + g * LANES:GLA_V + (g + 1) * LANES] = (
                ug[rows] * s * gate[rows]).astype(o_ref.dtype)
        yield


def _out_ln_pieces(a_ref, w_ref, x_ref, g_ref, b_ref, o_ref, y_ref, sub, tn):
    n = w_ref.shape[1]
    for t in range(o_ref.shape[0] // sub):
        rows = slice(t * sub, (t + 1) * sub)
        for c in range(n // tn):
            cols = slice(c * tn, (c + 1) * tn)
            y_ref[rows, cols] = _dot(a_ref[rows, :], w_ref[:, cols])
            yield
        r = ALPHA * x_ref[rows, :] + y_ref[rows, :]
        mu = jnp.mean(r, axis=-1, keepdims=True)
        rc = r - mu
        var = jnp.mean(rc * rc, axis=-1, keepdims=True)
        o_ref[rows, :] = rc * lax.rsqrt(var + LN_EPS) * g_ref[...] + b_ref[...]
        yield


def _even_layer_kernel(h_ref, r_ref, wg_ref, bg_ref, gn_ref, lng_ref, lnb_ref, ws_ref, bst_ref,
                       wo_ref, x_ref, pg_ref, pb_ref, o_ref, state_ref, wt_ref, bias_ref, mix_ref, y_ref,
                       *, ts, steps, sub):
    t = pl.program_id(0)

    @pl.when(t == 0)
    def _init():
        state_ref[...] = jnp.zeros_like(state_ref)
        mix_ref[1] = jnp.zeros(mix_ref.shape[1:], mix_ref.dtype)
        r = lax.broadcasted_iota(jnp.int32, (SGU_CHUNK, SGU_CHUNK), 0)
        c = lax.broadcasted_iota(jnp.int32, (SGU_CHUNK, SGU_CHUNK), 1)
        for g in range(SGU_GROUPS):
            wt_ref[g] = jnp.where(c <= r, ws_ref[g], 0.0).astype(BF16)
            bias_ref[g] = jnp.broadcast_to(bst_ref[:, g:g + 1], (SGU_CHUNK, LANES))

    def projection(slot):
        return _out_ln_pieces(mix_ref.at[slot], wo_ref, x_ref, pg_ref, pb_ref, o_ref, y_ref, sub, OUT_TN)

    def project(slot):
        for _ in projection(slot):
            pass

    def both(slot):
        proj = projection(1 - slot)
        for _ in _mix_pieces(h_ref, r_ref, wg_ref, bg_ref, gn_ref, lng_ref, lnb_ref, mix_ref.at[slot],
                             state_ref, wt_ref, bias_ref, ts):
            next(proj, None)
        for _ in proj:
            pass

    pl.when(jnp.logical_and(t < steps, t % 2 == 0))(functools.partial(both, 0))
    pl.when(jnp.logical_and(t < steps, t % 2 == 1))(functools.partial(both, 1))
    pl.when(t == steps)(functools.partial(project, (steps - 1) % 2))


def _even_layer(h3, r3, wg, bg, gn, lng, lnb, ws, bst, wo, x, pg, pb, ts, sub):
    nblk, s, _ = h3.shape
    steps = s // ts
    d = wo.shape[1]
    cur = lambda t: (0, jnp.minimum(t, steps - 1), 0)
    prev = lambda t: (jnp.maximum(t - 1, 0), 0)
    const2 = lambda t: (0, 0)
    return pl.pallas_call(
        functools.partial(_even_layer_kernel, ts=ts, steps=steps, sub=sub),
        out_shape=jax.ShapeDtypeStruct((s, d), F32),
        grid=(steps + 1,),
        in_specs=[pl.BlockSpec((nblk, ts, LANES), cur),
                  pl.BlockSpec((1, ts, LANES), cur),
                  pl.BlockSpec(wg.shape, const2),
                  pl.BlockSpec(bg.shape, const2),
                  pl.BlockSpec(gn.shape, const2),
                  pl.BlockSpec(lng.shape, const2),
                  pl.BlockSpec(lnb.shape, const2),
                  pl.BlockSpec(ws.shape, lambda t: (0, 0, 0)),
                  pl.BlockSpec(bst.shape, const2),
                  pl.BlockSpec(wo.shape, const2),
                  pl.BlockSpec((ts, d), prev),
                  pl.BlockSpec(pg.shape, const2),
                  pl.BlockSpec(pb.shape, const2)],
        out_specs=pl.BlockSpec((ts, d), prev),
        scratch_shapes=[pltpu.VMEM((GLA_HEADS, GLA_DV, GLA_DK), F32),
                        pltpu.VMEM((SGU_GROUPS, SGU_CHUNK, SGU_CHUNK), BF16),
                        pltpu.VMEM((SGU_GROUPS, SGU_CHUNK, LANES), F32),
                        pltpu.VMEM((2, ts, GLA_V + SGU_W), BF16),
                        pltpu.VMEM((ts, d), F32)],
        compiler_params=pltpu.CompilerParams(
            dimension_semantics=("arbitrary",), vmem_limit_bytes=VMEM_LIMIT),
        name="even_layer",
    )(h3, r3, wg, bg, gn, lng, lnb, ws, bst, wo, x, pg, pb)


def _sb_kernel(q_ref, kd_ref, kp_ref, vd_ref, vp_ref, g_ref, sfx_ref, kv_hbm, o_ref,
               acc_ref, carry_ref, kbuf_ref, vbuf_ref, sem, *, blk, grp):
    i = pl.program_id(0)
    nh = q_ref.shape[0]
    r = lax.broadcasted_iota(jnp.int32, (blk, blk), 0)
    c = lax.broadcasted_iota(jnp.int32, (blk, blk), 1)
    strict = c < r
    lanes = blk // LANES

    def softplus2(z2):
        return jnp.maximum(z2, 0.0) + jnp.log(1.0 + jnp.exp2(-jnp.abs(z2))) * LOG2E

    has_prev = i > 0
    for g0 in range(0, nh, grp):
        heads = range(g0, g0 + grp)
        z_d = [_dot_nt(q_ref[hd], kd_ref[hd]) for hd in heads]
        z_p = [_dot_nt(q_ref[hd], kp_ref[hd]) for hd in heads]
        a_d = [jnp.where(strict, softplus2(z), 0.0).astype(BF16) for z in z_d]
        a_p = [softplus2(z).astype(BF16) for z in z_p]
        cum = _dot(jnp.concatenate(a_d + a_p, axis=0), sfx_ref[...])
        for n, hd in enumerate(heads):
            c_d = cum[n * blk:(n + 1) * blk]
            c_p = cum[(grp + n) * blk:(grp + n + 1) * blk]
            tot_d = jnp.broadcast_to(c_d[:, 0:1], (blk, LANES))
            tot_p = jnp.broadcast_to(c_p[:, 0:1], (blk, LANES))
            w_d = jnp.where(strict, jnp.exp2(z_d[n] - c_d), 0.0).astype(BF16)
            w_p = jnp.exp2(z_p[n] - c_p - jnp.concatenate([tot_d] * lanes, axis=1)).astype(BF16)
            pv_d = _dot(w_d, vd_ref[hd])
            pv_p = _dot(w_p, vp_ref[hd])
            acc_ref[hd] = pv_d + jnp.where(has_prev, pv_p, 0.0)
            carry_ref[hd] = tot_d + tot_p

    def more(state):
        j, low = state
        return jnp.logical_and(j >= 0, low < F32_EXP2_UNDERFLOW)

    def copies(j, hd):
        rows = pl.ds(pl.multiple_of(j * blk, blk), blk)
        return (pltpu.make_async_copy(kv_hbm.at[SB_HEADS + hd, rows, :], kbuf_ref.at[hd], sem.at[0, hd]),
                pltpu.make_async_copy(kv_hbm.at[2 * SB_HEADS + hd, rows, :], vbuf_ref.at[hd], sem.at[1, hd]))

    def body(state):
        j, _ = state
        for hd in range(nh):
            for cp in copies(j, hd):
                cp.start()
        for hd in range(nh):
            for cp in copies(j, hd):
                cp.wait()
            z2 = _dot_nt(q_ref[hd], kbuf_ref[hd])
            cum = _dot(softplus2(z2).astype(BF16), sfx_ref[...])
            x = z2 - cum - jnp.concatenate([carry_ref[hd]] * lanes, axis=1)
            acc_ref[hd] += _dot(jnp.exp2(x).astype(BF16), vbuf_ref[hd])
            carry_ref[hd] += jnp.broadcast_to(cum[:, 0:1], (blk, LANES))
        return j - 1, jnp.min(carry_ref[...])

    lax.while_loop(more, body, (i - 2, jnp.min(carry_ref[...])))
    for hd in range(nh):
        o_ref[:, hd * SB_DIM:(hd + 1) * SB_DIM] = (
            acc_ref[hd] * _silu(g_ref[hd].astype(F32))).astype(o_ref.dtype)


def _stick_breaking(h3, blk, grp):
    _, s, _ = h3.shape
    nh = SB_HEADS
    rs = lax.broadcasted_iota(jnp.int32, (blk, blk), 0)
    cs = lax.broadcasted_iota(jnp.int32, (blk, blk), 1)
    sfx = jnp.where(rs >= cs, 1.0, 0.0).astype(BF16)
    here = lambda part: (lambda i: (part, i, 0))
    before = lambda part: (lambda i: (part, jnp.maximum(i - 1, 0), 0))
    head_block = (nh, blk, SB_DIM)
    return pl.pallas_call(
        functools.partial(_sb_kernel, blk=blk, grp=grp),
        out_shape=jax.ShapeDtypeStruct((s, nh * SB_DIM), BF16),
        grid=(s // blk,),
        in_specs=[pl.BlockSpec(head_block, here(0)),
                  pl.BlockSpec(head_block, here(1)),
                  pl.BlockSpec(head_block, before(1)),
                  pl.BlockSpec(head_block, here(2)),
                  pl.BlockSpec(head_block, before(2)),
                  pl.BlockSpec(head_block, here(3)),
                  pl.BlockSpec(sfx.shape, lambda i: (0, 0)),
                  pl.BlockSpec(memory_space=pl.ANY)],
        out_specs=pl.BlockSpec((blk, nh * SB_DIM), lambda i: (i, 0)),
        scratch_shapes=[pltpu.VMEM((nh, blk, SB_DIM), F32), pltpu.VMEM((nh, blk, LANES), F32),
                        pltpu.VMEM(head_block, BF16), pltpu.VMEM(head_block, BF16),
                        pltpu.SemaphoreType.DMA((2, nh))],
        compiler_params=pltpu.CompilerParams(
            dimension_semantics=("arbitrary",), vmem_limit_bytes=VMEM_LIMIT),
        name="stick_breaking",
    )(h3, h3, h3, h3, h3, h3, sfx, h3)


def _pick(n, candidates):
    for c in candidates:
        if n % c == 0:
            return c
    raise ValueError(f"no tile of {candidates} divides {n}")


def kernel(x, even_w_in, even_gla_w_gate2, even_gla_b_gate, even_gla_norm_g, even_sgu_ln_g, even_sgu_ln_b,
           even_sgu_w_s, even_sgu_b_s, even_w_out, odd_w_in, odd_w_out, post_ln_g, post_ln_b):
    bsz, seq, d = x.shape
    assert bsz == 1 and d == D_MODEL
    xs = x[0]
    tm = _pick(seq, (1024, 512, 256))
    tl = _pick(seq, (256,))
    to = _pick(seq, (1024, 512, 256))

    wt = jnp.swapaxes(even_w_in[0], 0, 1).astype(BF16)
    o_r = 2 * GLA_QK + GLA_V
    assert o_r % EVEN_TN == 0
    w_rank = jnp.concatenate([wt[o_r:o_r + GLA_RANK], jnp.zeros((LANES - GLA_RANK, d), BF16)], axis=0)
    wg = jnp.concatenate([even_gla_w_gate2[0], jnp.zeros((LANES - GLA_RANK, GLA_QK), F32)], axis=0).astype(BF16)
    h3, r3 = _project(xs, wt, jnp.ones((1, EVEN_BLOCKS * LANES), F32), tm, EVEN_TN, True,
                      row_start=lambda j: j * EVEN_TN + jnp.where(j * EVEN_TN >= o_r, GLA_RANK, 0),
                      side_w=w_rank)
    x1 = _even_layer(h3, r3, wg, even_gla_b_gate[0][None, :], even_gla_norm_g[0][None, :],
                     even_sgu_ln_g[0][None, :], even_sgu_ln_b[0][None, :],
                     even_sgu_w_s[0], even_sgu_b_s[0].T, even_w_out[0].astype(BF16), xs,
                     post_ln_g[0][None, :], post_ln_b[0][None, :], tl, OUT_SUB)

    q_scale = jnp.concatenate([jnp.full((1, SB_HEADS * SB_DIM), SB_DIM ** -0.5 * LOG2E, F32),
                               jnp.ones((1, 3 * SB_HEADS * SB_DIM), F32)], axis=1)
    h3, = _project(x1, odd_w_in[0].astype(BF16), q_scale, tm, 1024, False)
    att = _stick_breaking(h3, 256, 4)
    x2 = _out_proj_ln(att, odd_w_out[0].astype(BF16), x1, post_ln_g[1][None, :], post_ln_b[1][None, :], to, OUT_SUB)
    return x2[None]
```

```python
import functools

import jax
import jax.numpy as jnp
from jax import lax
from jax.experimental import pallas as pl
from jax.experimental.pallas import tpu as pltpu

F32 = jnp.float32
BF16 = jnp.bfloat16

LANES = 128
BF16_SUBLANES = 16
D_MODEL = 2048
DEPTH = 2
GLA_HEADS = 4
GLA_DK = 128
GLA_DV = 256
GLA_QK = GLA_HEADS * GLA_DK
GLA_V = GLA_HEADS * GLA_DV
GLA_RANK = 16
GLA_TAU = 16.0
GLA_CHUNK = 64
SGU_W = 1024
SGU_CHUNK = 128
SGU_GROUPS = SGU_W // LANES
SB_HEADS = 16
SB_DIM = 128
ALPHA = (2.0 * DEPTH) ** 0.25
LN_EPS = 1e-5

_QA, _KA, _VA, _GA, _UB, _VB, _GB = 0, 4, 8, 16, 24, 32, 40
EVEN_BLOCKS = 48
EVEN_TN = 1024
SB_BLOCK = 256
SB_GROUP = 4
OUT_TN = 256
EVEN_OUT_SUB = 128
ODD_OUT_SUB = 256

VMEM_LIMIT = 56 * 1024 * 1024
LOG2E = 1.4426950408889634
F32_EXP2_UNDERFLOW = 151.0


def _softplus(x):
    return jnp.maximum(x, 0.0) + jnp.log(1.0 + jnp.exp(-jnp.abs(x)))


def _softplus2(z2):
    return jnp.maximum(z2, 0.0) + jnp.log(1.0 + jnp.exp2(-jnp.abs(z2))) * LOG2E


def _silu(x):
    return x / (1.0 + jnp.exp2(x * -LOG2E))


def _gelu_tanh(x):
    k1 = -2.0 * 0.7978845608028654 * LOG2E
    return x / (1.0 + jnp.exp2(x * (k1 + (k1 * 0.044715) * (x * x))))


def _split_bf16(x):
    hi = x.astype(BF16)
    lo = (x - hi.astype(F32)).astype(BF16)
    return hi, lo


def _dot(a, b):
    return jnp.dot(a, b, preferred_element_type=F32)


def _dot_nt(a, b):
    return lax.dot_general(a, b, (((1,), (1,)), ((), ())), preferred_element_type=F32)


def _proj_kernel(*refs, w_transposed, has_side):
    if has_side:
        a_ref, w_ref, cs_ref, ws_ref, o_ref, os_ref, ab_ref = refs
    else:
        a_ref, w_ref, cs_ref, o_ref, ab_ref = refs

    @pl.when(pl.program_id(1) == 0)
    def _first():
        ab_ref[...] = a_ref[...].astype(BF16)
        if has_side:
            os_ref[0] = _dot_nt(ab_ref[...], ws_ref[...]).astype(os_ref.dtype)

    dot = _dot_nt if w_transposed else _dot
    r = dot(ab_ref[...], w_ref[...]) * cs_ref[...]
    for c in range(o_ref.shape[0]):
        o_ref[c] = r[:, c * LANES:(c + 1) * LANES].astype(o_ref.dtype)


def _project(a, w, col_scale, tm, tn, w_transposed, row_start=None, side_w=None):
    m, k = a.shape
    n = col_scale.shape[1]
    nb = tn // LANES
    if not w_transposed:
        w_spec = pl.BlockSpec((k, tn), lambda i, j: (0, j))
    elif row_start is None:
        w_spec = pl.BlockSpec((tn, k), lambda i, j: (j, 0))
    else:
        w_spec = pl.BlockSpec((pl.Element(tn), pl.Element(k)),
                              lambda i, j: (pl.multiple_of(row_start(j), BF16_SUBLANES), 0))
    in_specs = [pl.BlockSpec((tm, k), lambda i, j: (i, 0)), w_spec, pl.BlockSpec((1, tn), lambda i, j: (0, j))]
    out_shape = [jax.ShapeDtypeStruct((n // LANES, m, LANES), BF16)]
    out_specs = [pl.BlockSpec((nb, tm, LANES), lambda i, j: (j, i, 0))]
    operands = [a, w, col_scale]
    if side_w is not None:
        in_specs.append(pl.BlockSpec(side_w.shape, lambda i, j: (0, 0)))
        out_shape.append(jax.ShapeDtypeStruct((1, m, LANES), BF16))
        out_specs.append(pl.BlockSpec((1, tm, LANES), lambda i, j: (0, i, 0)))
        operands.append(side_w)
    return pl.pallas_call(
        functools.partial(_proj_kernel, w_transposed=w_transposed, has_side=side_w is not None),
        out_shape=out_shape,
        grid=(m // tm, n // tn),
        in_specs=in_specs,
        out_specs=out_specs,
        scratch_shapes=[pltpu.VMEM((tm, k), BF16)],
        compiler_params=pltpu.CompilerParams(
            dimension_semantics=("arbitrary", "arbitrary"), vmem_limit_bytes=VMEM_LIMIT),
        name="in_proj",
    )(*operands)


def _out_ln_pieces(a_ref, w_ref, x_ref, g_ref, b_ref, o_ref, y_ref, sub, tn):
    n = w_ref.shape[1]
    for t in range(o_ref.shape[0] // sub):
        rows = slice(t * sub, (t + 1) * sub)
        for c in range(n // tn):
            cols = slice(c * tn, (c + 1) * tn)
            y_ref[rows, cols] = _dot(a_ref[rows, :], w_ref[:, cols])
            yield
        r = ALPHA * x_ref[rows, :] + y_ref[rows, :]
        mu = jnp.mean(r, axis=-1, keepdims=True)
        rc = r - mu
        var = jnp.mean(rc * rc, axis=-1, keepdims=True)
        o_ref[rows, :] = rc * lax.rsqrt(var + LN_EPS) * g_ref[...] + b_ref[...]
        yield


def _interleave(main, side, n_main=1, n_side=1):
    for k, _ in enumerate(main):
        for _ in range((k + 1) * n_side // n_main - k * n_side // n_main):
            next(side, None)
    for _ in side:
        pass


def _mix_pieces(h_ref, r_ref, wg_ref, bg_ref, gn_ref, lng_ref, lnb_ref, o_ref, state_ref, wt_ref, bias_ref, ts):
    u = _dot(r_ref[0], wg_ref[...]) + bg_ref[...]
    log_a = -_softplus(-u) * (1.0 / GLA_TAU)
    r = lax.broadcasted_iota(jnp.int32, (ts, ts), 0)
    c = lax.broadcasted_iota(jnp.int32, (ts, ts), 1)
    same_chunk = (r // GLA_CHUNK) == (c // GLA_CHUNK)
    cum_mat = jnp.where(same_chunk & (c <= r), 1.0, 0.0).astype(BF16)
    la_hi, la_lo = _split_bf16(log_a)
    b_all = _dot(cum_mat, la_hi) + _dot(cum_mat, la_lo)

    rr = lax.broadcasted_iota(jnp.int32, (GLA_CHUNK, GLA_CHUNK), 0)
    cc = lax.broadcasted_iota(jnp.int32, (GLA_CHUNK, GLA_CHUNK), 1)
    causal = cc <= rr
    scale = GLA_DK ** -0.5
    gn = gn_ref[...]

    for ck in range(ts // GLA_CHUNK):
        rows = slice(ck * GLA_CHUNK, (ck + 1) * GLA_CHUNK)
        bc = b_all[rows]
        b_mid = bc[GLA_CHUNK // 2:GLA_CHUNK // 2 + 1]
        b_last = bc[GLA_CHUNK - 1:GLA_CHUNK]
        e_q = jnp.exp(bc - b_mid) * scale
        e_k = jnp.exp(b_mid - bc)
        e_kv = jnp.exp(b_last - bc)
        e_b = jnp.exp(bc) * scale
        decay = jnp.exp(b_last)
        for hd in range(GLA_HEADS):
            sl = slice(hd * GLA_DK, (hd + 1) * GLA_DK)
            qh = h_ref[_QA + hd, rows, :].astype(F32)
            kh = h_ref[_KA + hd, rows, :].astype(F32)
            vh = jnp.concatenate([h_ref[_VA + 2 * hd, rows, :], h_ref[_VA + 2 * hd + 1, rows, :]], axis=1)
            sc = _dot_nt((qh * e_q[:, sl]).astype(BF16), (kh * e_k[:, sl]).astype(BF16))
            sc = jnp.where(causal, sc, 0.0).astype(BF16)
            o = _dot(sc, vh)
            st = state_ref[hd]
            o = o + _dot_nt((qh * e_b[:, sl]).astype(BF16), st.astype(BF16))
            kk = (kh * e_kv[:, sl]).astype(BF16)
            kv_t = lax.dot_general(vh, kk, (((0,), (0,)), ((), ())), preferred_element_type=F32)
            state_ref[hd] = st * decay[:, sl] + kv_t
            ms = jnp.mean(o * o, axis=-1, keepdims=True)
            on = o * lax.rsqrt(ms + LN_EPS) * gn
            ga = jnp.concatenate([h_ref[_GA + 2 * hd, rows, :], h_ref[_GA + 2 * hd + 1, rows, :]],
                                 axis=1).astype(F32)
            o_ref[rows, hd * GLA_DV:(hd + 1) * GLA_DV] = (on * _silu(ga)).astype(o_ref.dtype)
            yield

    vg = [_gelu_tanh(h_ref[_VB + g].astype(F32)) for g in range(SGU_GROUPS)]
    tot = vg[0]
    for g in range(1, SGU_GROUPS):
        tot = tot + vg[g]
    mu = jnp.sum(tot, axis=-1, keepdims=True) * (1.0 / SGU_W)
    vc = [v - mu for v in vg]
    sq = vc[0] * vc[0]
    for g in range(1, SGU_GROUPS):
        sq = sq + vc[g] * vc[g]
    rstd = lax.rsqrt(jnp.sum(sq, axis=-1, keepdims=True) * (1.0 / SGU_W) + LN_EPS)
    for g in range(SGU_GROUPS):
        cols = slice(g * LANES, (g + 1) * LANES)
        vn = (vc[g] * rstd * lng_ref[:, cols] + lnb_ref[:, cols]).astype(BF16)
        ug = _gelu_tanh(h_ref[_UB + g].astype(F32))
        gate = _silu(h_ref[_GB + g].astype(F32))
        for ck in range(ts // SGU_CHUNK):
            rows = slice(ck * SGU_CHUNK, (ck + 1) * SGU_CHUNK)
            s = _dot(wt_ref[g], vn[rows]) + bias_ref[g]
            o_ref[rows, GLA_V + g * LANES:GLA_V + (g + 1) * LANES] = (
                ug[rows] * s * gate[rows]).astype(o_ref.dtype)
        yield


def _even_layer_kernel(h_ref, r_ref, wg_ref, bg_ref, gn_ref, lng_ref, lnb_ref, ws_ref, bst_ref,
                       wo_ref, x_ref, pg_ref, pb_ref, o_ref, state_ref, wt_ref, bias_ref, mix_ref, y_ref,
                       *, ts, steps, sub):
    t = pl.program_id(0)

    @pl.when(t == 0)
    def _init():
        state_ref[...] = jnp.zeros_like(state_ref)
        mix_ref[1] = jnp.zeros(mix_ref.shape[1:], mix_ref.dtype)
        r = lax.broadcasted_iota(jnp.int32, (SGU_CHUNK, SGU_CHUNK), 0)
        c = lax.broadcasted_iota(jnp.int32, (SGU_CHUNK, SGU_CHUNK), 1)
        for g in range(SGU_GROUPS):
            wt_ref[g] = jnp.where(c <= r, ws_ref[g], 0.0).astype(BF16)
            bias_ref[g] = jnp.broadcast_to(bst_ref[:, g:g + 1], (SGU_CHUNK, LANES))

    def projection(slot):
        return _out_ln_pieces(mix_ref.at[slot], wo_ref, x_ref, pg_ref, pb_ref, o_ref, y_ref, sub, OUT_TN)

    def both(slot):
        n_mix = (ts // GLA_CHUNK) * GLA_HEADS + SGU_GROUPS
        n_proj = (ts // sub) * (wo_ref.shape[1] // OUT_TN + 1)
        _interleave(_mix_pieces(h_ref, r_ref, wg_ref, bg_ref, gn_ref, lng_ref, lnb_ref, mix_ref.at[slot],
                                state_ref, wt_ref, bias_ref, ts),
                    projection(1 - slot), n_mix, n_proj)

    pl.when(jnp.logical_and(t < steps, t % 2 == 0))(functools.partial(both, 0))
    pl.when(jnp.logical_and(t < steps, t % 2 == 1))(functools.partial(both, 1))
    pl.when(t == steps)(lambda: _interleave((), projection((steps - 1) % 2)))


def _even_layer(h3, r3, wg, bg, gn, lng, lnb, ws, bst, wo, x, pg, pb, ts, sub):
    nblk, s, _ = h3.shape
    steps = s // ts
    d = wo.shape[1]
    cur = lambda t: (0, jnp.minimum(t, steps - 1), 0)
    prev = lambda t: (jnp.maximum(t - 1, 0), 0)
    const2 = lambda t: (0, 0)
    return pl.pallas_call(
        functools.partial(_even_layer_kernel, ts=ts, steps=steps, sub=sub),
        out_shape=jax.ShapeDtypeStruct((s, d), F32),
        grid=(steps + 1,),
        in_specs=[pl.BlockSpec((nblk, ts, LANES), cur),
                  pl.BlockSpec((1, ts, LANES), cur),
                  pl.BlockSpec(wg.shape, const2),
                  pl.BlockSpec(bg.shape, const2),
                  pl.BlockSpec(gn.shape, const2),
                  pl.BlockSpec(lng.shape, const2),
                  pl.BlockSpec(lnb.shape, const2),
                  pl.BlockSpec(ws.shape, lambda t: (0, 0, 0)),
                  pl.BlockSpec(bst.shape, const2),
                  pl.BlockSpec(wo.shape, const2),
                  pl.BlockSpec((ts, d), prev),
                  pl.BlockSpec(pg.shape, const2),
                  pl.BlockSpec(pb.shape, const2)],
        out_specs=pl.BlockSpec((ts, d), prev),
        scratch_shapes=[pltpu.VMEM((GLA_HEADS, GLA_DV, GLA_DK), F32),
                        pltpu.VMEM((SGU_GROUPS, SGU_CHUNK, SGU_CHUNK), BF16),
                        pltpu.VMEM((SGU_GROUPS, SGU_CHUNK, LANES), F32),
                        pltpu.VMEM((2, ts, GLA_V + SGU_W), BF16),
                        pltpu.VMEM((ts, d), F32)],
        compiler_params=pltpu.CompilerParams(
            dimension_semantics=("arbitrary",), vmem_limit_bytes=VMEM_LIMIT),
        name="even_layer",
    )(h3, r3, wg, bg, gn, lng, lnb, ws, bst, wo, x, pg, pb)


def _sb_pieces(i, q_ref, kd_ref, kp_ref, vd_ref, vp_ref, sfx_ref, acc_ref, carry_ref, blk, grp):
    nh = q_ref.shape[0]
    r = lax.broadcasted_iota(jnp.int32, (blk, blk), 0)
    c = lax.broadcasted_iota(jnp.int32, (blk, blk), 1)
    strict = c < r
    lanes = blk // LANES
    has_prev = i > 0
    for g0 in range(0, nh, grp):
        heads = range(g0, g0 + grp)
        z_d = [_dot_nt(q_ref[hd], kd_ref[hd]) for hd in heads]
        z_p = [_dot_nt(q_ref[hd], kp_ref[hd]) for hd in heads]
        yield
        a_d = [jnp.where(strict, _softplus2(z), 0.0).astype(BF16) for z in z_d]
        yield
        a_p = [_softplus2(z).astype(BF16) for z in z_p]
        yield
        cum = _dot(jnp.concatenate(a_d + a_p, axis=0), sfx_ref[...])
        for n, hd in enumerate(heads):
            c_d = cum[n * blk:(n + 1) * blk]
            c_p = cum[(grp + n) * blk:(grp + n + 1) * blk]
            tot_d = jnp.broadcast_to(c_d[:, 0:1], (blk, LANES))
            tot_p = jnp.broadcast_to(c_p[:, 0:1], (blk, LANES))
            w_d = jnp.where(strict, jnp.exp2(z_d[n] - c_d), 0.0).astype(BF16)
            w_p = jnp.exp2(z_p[n] - c_p - jnp.concatenate([tot_d] * lanes, axis=1)).astype(BF16)
            pv_d = _dot(w_d, vd_ref[hd])
            pv_p = _dot(w_p, vp_ref[hd])
            acc_ref[hd] = pv_d + jnp.where(has_prev, pv_p, 0.0)
            carry_ref[hd] = tot_d + tot_p
            yield


def _sb_deeper(i, q_ref, sfx_ref, kv_hbm, acc_ref, carry_ref, kbuf_ref, vbuf_ref, sem, blk):
    nh = q_ref.shape[0]
    lanes = blk // LANES

    def more(state):
        j, low = state
        return jnp.logical_and(j >= 0, low < F32_EXP2_UNDERFLOW)

    def copies(j, hd):
        rows = pl.ds(pl.multiple_of(j * blk, blk), blk)
        return (pltpu.make_async_copy(kv_hbm.at[SB_HEADS + hd, rows, :], kbuf_ref.at[hd], sem.at[0, hd]),
                pltpu.make_async_copy(kv_hbm.at[2 * SB_HEADS + hd, rows, :], vbuf_ref.at[hd], sem.at[1, hd]))

    def body(state):
        j, _ = state
        for hd in range(nh):
            for cp in copies(j, hd):
                cp.start()
        for hd in range(nh):
            for cp in copies(j, hd):
                cp.wait()
            z2 = _dot_nt(q_ref[hd], kbuf_ref[hd])
            cum = _dot(_softplus2(z2).astype(BF16), sfx_ref[...])
            x = z2 - cum - jnp.concatenate([carry_ref[hd]] * lanes, axis=1)
            acc_ref[hd] += _dot(jnp.exp2(x).astype(BF16), vbuf_ref[hd])
            carry_ref[hd] += jnp.broadcast_to(cum[:, 0:1], (blk, LANES))
        return j - 1, jnp.min(carry_ref[...])

    lax.while_loop(more, body, (i - 2, jnp.min(carry_ref[...])))


def _odd_layer_kernel(q_ref, kd_ref, kp_ref, vd_ref, vp_ref, g_ref, sfx_ref, kv_hbm, wo_ref, x_ref, pg_ref, pb_ref,
                      o_ref, acc_ref, carry_ref, kbuf_ref, vbuf_ref, sem, att_ref, y_ref, *, blk, grp, steps, sub):
    t = pl.program_id(0)

    @pl.when(t == 0)
    def _init():
        att_ref[1] = jnp.zeros(att_ref.shape[1:], att_ref.dtype)

    def projection(slot):
        return _out_ln_pieces(att_ref.at[slot], wo_ref, x_ref, pg_ref, pb_ref, o_ref, y_ref, sub, OUT_TN)

    def both(slot):
        n_att = (q_ref.shape[0] // grp) * (3 + grp)
        n_proj = (blk // sub) * (wo_ref.shape[1] // OUT_TN + 1)
        _interleave(_sb_pieces(t, q_ref, kd_ref, kp_ref, vd_ref, vp_ref, sfx_ref, acc_ref, carry_ref, blk, grp),
                    projection(1 - slot), n_att, n_proj)
        _sb_deeper(t, q_ref, sfx_ref, kv_hbm, acc_ref, carry_ref, kbuf_ref, vbuf_ref, sem, blk)
        for hd in range(q_ref.shape[0]):
            att_ref[slot, :, hd * SB_DIM:(hd + 1) * SB_DIM] = (
                acc_ref[hd] * _silu(g_ref[hd].astype(F32))).astype(att_ref.dtype)

    pl.when(jnp.logical_and(t < steps, t % 2 == 0))(functools.partial(both, 0))
    pl.when(jnp.logical_and(t < steps, t % 2 == 1))(functools.partial(both, 1))
    pl.when(t == steps)(lambda: _interleave((), projection((steps - 1) % 2)))


def _odd_layer(h3, wo, x, pg, pb, blk, grp, sub):
    _, s, _ = h3.shape
    nh = SB_HEADS
    d = wo.shape[1]
    steps = s // blk
    rs = lax.broadcasted_iota(jnp.int32, (blk, blk), 0)
    cs = lax.broadcasted_iota(jnp.int32, (blk, blk), 1)
    sfx = jnp.where(rs >= cs, 1.0, 0.0).astype(BF16)
    here = lambda part: (lambda t: (part, jnp.minimum(t, steps - 1), 0))
    before = lambda part: (lambda t: (part, jnp.maximum(jnp.minimum(t, steps - 1) - 1, 0), 0))
    prev = lambda t: (jnp.maximum(t - 1, 0), 0)
    const2 = lambda t: (0, 0)
    head_block = (nh, blk, SB_DIM)
    return pl.pallas_call(
        functools.partial(_odd_layer_kernel, blk=blk, grp=grp, steps=steps, sub=sub),
        out_shape=jax.ShapeDtypeStruct((s, d), F32),
        grid=(steps + 1,),
        in_specs=[pl.BlockSpec(head_block, here(0)),
                  pl.BlockSpec(head_block, here(1)),
                  pl.BlockSpec(head_block, before(1)),
                  pl.BlockSpec(head_block, here(2)),
                  pl.BlockSpec(head_block, before(2)),
                  pl.BlockSpec(head_block, here(3)),
                  pl.BlockSpec(sfx.shape, const2),
                  pl.BlockSpec(memory_space=pl.ANY),
                  pl.BlockSpec(wo.shape, const2),
                  pl.BlockSpec((blk, d), prev),
                  pl.BlockSpec(pg.shape, const2),
                  pl.BlockSpec(pb.shape, const2)],
        out_specs=pl.BlockSpec((blk, d), prev),
        scratch_shapes=[pltpu.VMEM((nh, blk, SB_DIM), F32), pltpu.VMEM((nh, blk, LANES), F32),
                        pltpu.VMEM(head_block, BF16), pltpu.VMEM(head_block, BF16),
                        pltpu.SemaphoreType.DMA((2, nh)),
                        pltpu.VMEM((2, blk, nh * SB_DIM), BF16), pltpu.VMEM((blk, d), F32)],
        compiler_params=pltpu.CompilerParams(
            dimension_semantics=("arbitrary",), vmem_limit_bytes=VMEM_LIMIT),
        name="odd_layer",
    )(h3, h3, h3, h3, h3, h3, sfx, h3, wo, x, pg, pb)


def _pick(n, candidates):
    for c in candidates:
        if n % c == 0:
            return c
    raise ValueError(f"no tile of {candidates} divides {n}")


def kernel(x, even_w_in, even_gla_w_gate2, even_gla_b_gate, even_gla_norm_g, even_sgu_ln_g, even_sgu_ln_b,
           even_sgu_w_s, even_sgu_b_s, even_w_out, odd_w_in, odd_w_out, post_ln_g, post_ln_b):
    bsz, seq, d = x.shape
    assert bsz == 1 and d == D_MODEL
    xs = x[0]
    tm = _pick(seq, (1024, 512, 256))
    tl = _pick(seq, (256,))

    wt = jnp.swapaxes(even_w_in[0], 0, 1).astype(BF16)
    o_r = 2 * GLA_QK + GLA_V
    assert o_r % EVEN_TN == 0
    w_rank = jnp.concatenate([wt[o_r:o_r + GLA_RANK], jnp.zeros((LANES - GLA_RANK, d), BF16)], axis=0)
    wg = jnp.concatenate([even_gla_w_gate2[0], jnp.zeros((LANES - GLA_RANK, GLA_QK), F32)], axis=0).astype(BF16)
    h3, r3 = _project(xs, wt, jnp.ones((1, EVEN_BLOCKS * LANES), F32), tm, EVEN_TN, True,
                      row_start=lambda j: j * EVEN_TN + jnp.where(j * EVEN_TN >= o_r, GLA_RANK, 0),
                      side_w=w_rank)
    x1 = _even_layer(h3, r3, wg, even_gla_b_gate[0][None, :], even_gla_norm_g[0][None, :],
                     even_sgu_ln_g[0][None, :], even_sgu_ln_b[0][None, :],
                     even_sgu_w_s[0], even_sgu_b_s[0].T, even_w_out[0].astype(BF16), xs,
                     post_ln_g[0][None, :], post_ln_b[0][None, :], tl, EVEN_OUT_SUB)

    q_scale = jnp.concatenate([jnp.full((1, SB_HEADS * SB_DIM), SB_DIM ** -0.5 * LOG2E, F32),
                               jnp.ones((1, 3 * SB_HEADS * SB_DIM), F32)], axis=1)
    h3, = _project(x1, odd_w_in[0].astype(BF16), q_scale, tm, 1024, False)
    x2 = _odd_layer(h3, odd_w_out[0].astype(BF16), x1, post_ln_g[1][None, :], post_ln_b[1][None, :],
                    SB_BLOCK, SB_GROUP, ODD_OUT_SUB)
    return x2[None]
```

```python
import functools

import jax
import jax.numpy as jnp
from jax import lax
from jax.experimental import pallas as pl
from jax.experimental.pallas import tpu as pltpu

F32 = jnp.float32
BF16 = jnp.bfloat16

LANES = 128
SUBLANES = 8
D_MODEL = 2048
DEPTH = 2
GLA_HEADS = 4
GLA_DK = 128
GLA_DV = 256
GLA_QK = GLA_HEADS * GLA_DK
GLA_V = GLA_HEADS * GLA_DV
GLA_RANK = 16
GLA_TAU = 16.0
GLA_CHUNK = 64
SGU_W = 1024
SGU_CHUNK = 128
SGU_GROUPS = SGU_W // LANES
SB_HEADS = 16
SB_DIM = 128
ALPHA = (2.0 * DEPTH) ** 0.25
LN_EPS = 1e-5

_QA, _KA, _VA, _GA, _UB, _VB, _GB = 0, 4, 8, 16, 24, 32, 40
EVEN_BLOCKS = 48
EVEN_TN = 1024
SB_BLOCK = 256
SB_GROUP = 4
OUT_TN = 256
EVEN_OUT_SUB = 128
ODD_OUT_SUB = 256

VMEM_LIMIT = 56 * 1024 * 1024
LOG2E = 1.4426950408889634
F32_EXP2_UNDERFLOW = 151.0


def _softplus(x):
    return jnp.maximum(x, 0.0) + jnp.log(1.0 + jnp.exp(-jnp.abs(x)))


def _softplus2(z2):
    return jnp.maximum(z2, 0.0) + jnp.log(1.0 + jnp.exp2(-jnp.abs(z2))) * LOG2E


def _silu(x):
    return x / (1.0 + jnp.exp2(x * -LOG2E))


def _gelu_tanh(x):
    k1 = -2.0 * 0.7978845608028654 * LOG2E
    return x / (1.0 + jnp.exp2(x * (k1 + (k1 * 0.044715) * (x * x))))


def _split_bf16(x):
    hi = x.astype(BF16)
    lo = (x - hi.astype(F32)).astype(BF16)
    return hi, lo


def _dot(a, b):
    return jnp.dot(a, b, preferred_element_type=F32)


def _dot_nt(a, b):
    return lax.dot_general(a, b, (((1,), (1,)), ((), ())), preferred_element_type=F32)


def _proj_kernel(*refs, w_transposed, has_side):
    if has_side:
        a_ref, w_ref, cs_ref, ws_ref, o_ref, os_ref, ab_ref = refs
    else:
        a_ref, w_ref, cs_ref, o_ref, ab_ref = refs

    @pl.when(pl.program_id(1) == 0)
    def _first():
        ab_ref[...] = a_ref[...].astype(BF16)
        if has_side:
            os_ref[0] = _dot_nt(ab_ref[...], ws_ref[...]).astype(os_ref.dtype)

    dot = _dot_nt if w_transposed else _dot
    r = dot(ab_ref[...], w_ref[...].astype(BF16)) * cs_ref[...]
    for c in range(o_ref.shape[0]):
        o_ref[c] = r[:, c * LANES:(c + 1) * LANES].astype(o_ref.dtype)


def _project(a, w, col_scale, tm, tn, w_transposed, row_start=None, side_w=None):
    m, k = a.shape
    n = col_scale.shape[1]
    nb = tn // LANES
    if not w_transposed:
        w_spec = pl.BlockSpec((k, tn), lambda i, j: (0, j))
    elif row_start is None:
        w_spec = pl.BlockSpec((tn, k), lambda i, j: (j, 0))
    else:
        w_spec = pl.BlockSpec((pl.Element(tn), pl.Element(k)),
                              lambda i, j: (pl.multiple_of(row_start(j), SUBLANES), 0))
    in_specs = [pl.BlockSpec((tm, k), lambda i, j: (i, 0)), w_spec, pl.BlockSpec((1, tn), lambda i, j: (0, j))]
    out_shape = [jax.ShapeDtypeStruct((n // LANES, m, LANES), BF16)]
    out_specs = [pl.BlockSpec((nb, tm, LANES), lambda i, j: (j, i, 0))]
    operands = [a, w, col_scale]
    if side_w is not None:
        in_specs.append(pl.BlockSpec(side_w.shape, lambda i, j: (0, 0)))
        out_shape.append(jax.ShapeDtypeStruct((1, m, LANES), BF16))
        out_specs.append(pl.BlockSpec((1, tm, LANES), lambda i, j: (0, i, 0)))
        operands.append(side_w)
    return pl.pallas_call(
        functools.partial(_proj_kernel, w_transposed=w_transposed, has_side=side_w is not None),
        out_shape=out_shape,
        grid=(m // tm, n // tn),
        in_specs=in_specs,
        out_specs=out_specs,
        scratch_shapes=[pltpu.VMEM((tm, k), BF16)],
        compiler_params=pltpu.CompilerParams(
            dimension_semantics=("arbitrary", "arbitrary"), vmem_limit_bytes=VMEM_LIMIT),
        name="in_proj",
    )(*operands)


def _out_ln_pieces(a_ref, w_ref, x_ref, g_ref, b_ref, o_ref, y_ref, sub, tn):
    n = w_ref.shape[1]
    for t in range(o_ref.shape[0] // sub):
        rows = slice(t * sub, (t + 1) * sub)
        for c in range(n // tn):
            cols = slice(c * tn, (c + 1) * tn)
            y_ref[rows, cols] = _dot(a_ref[rows, :], w_ref[:, cols])
            yield
        r = ALPHA * x_ref[rows, :] + y_ref[rows, :]
        mu = jnp.mean(r, axis=-1, keepdims=True)
        rc = r - mu
        var = jnp.mean(rc * rc, axis=-1, keepdims=True)
        o_ref[rows, :] = rc * lax.rsqrt(var + LN_EPS) * g_ref[...] + b_ref[...]
        yield


def _interleave(main, side, n_main=1, n_side=1):
    for k, _ in enumerate(main):
        for _ in range((k + 1) * n_side // n_main - k * n_side // n_main):
            next(side, None)
    for _ in side:
        pass


def _mix_pieces(h_ref, r_ref, wg_ref, bg_ref, gn_ref, lng_ref, lnb_ref, o_ref, state_ref, wt_ref, bias_ref, ts):
    u = _dot(r_ref[0], wg_ref[...]) + bg_ref[...]
    log_a = -_softplus(-u) * (1.0 / GLA_TAU)
    r = lax.broadcasted_iota(jnp.int32, (ts, ts), 0)
    c = lax.broadcasted_iota(jnp.int32, (ts, ts), 1)
    same_chunk = (r // GLA_CHUNK) == (c // GLA_CHUNK)
    cum_mat = jnp.where(same_chunk & (c <= r), 1.0, 0.0).astype(BF16)
    la_hi, la_lo = _split_bf16(log_a)
    b_all = _dot(cum_mat, la_hi) + _dot(cum_mat, la_lo)

    rr = lax.broadcasted_iota(jnp.int32, (GLA_CHUNK, GLA_CHUNK), 0)
    cc = lax.broadcasted_iota(jnp.int32, (GLA_CHUNK, GLA_CHUNK), 1)
    causal = cc <= rr
    scale = GLA_DK ** -0.5
    gn = gn_ref[...]

    for ck in range(ts // GLA_CHUNK):
        rows = slice(ck * GLA_CHUNK, (ck + 1) * GLA_CHUNK)
        bc = b_all[rows]
        b_mid = bc[GLA_CHUNK // 2:GLA_CHUNK // 2 + 1]
        b_last = bc[GLA_CHUNK - 1:GLA_CHUNK]
        e_q = jnp.exp(bc - b_mid) * scale
        e_k = jnp.exp(b_mid - bc)
        e_kv = jnp.exp(b_last - bc)
        e_b = jnp.exp(bc) * scale
        decay = jnp.exp(b_last)
        for hd in range(GLA_HEADS):
            sl = slice(hd * GLA_DK, (hd + 1) * GLA_DK)
            qh = h_ref[_QA + hd, rows, :].astype(F32)
            kh = h_ref[_KA + hd, rows, :].astype(F32)
            vh = jnp.concatenate([h_ref[_VA + 2 * hd, rows, :], h_ref[_VA + 2 * hd + 1, rows, :]], axis=1)
            sc = _dot_nt((qh * e_q[:, sl]).astype(BF16), (kh * e_k[:, sl]).astype(BF16))
            sc = jnp.where(causal, sc, 0.0).astype(BF16)
            o = _dot(sc, vh)
            st = state_ref[hd]
            o = o + _dot_nt((qh * e_b[:, sl]).astype(BF16), st.astype(BF16))
            kk = (kh * e_kv[:, sl]).astype(BF16)
            kv_t = lax.dot_general(vh, kk, (((0,), (0,)), ((), ())), preferred_element_type=F32)
            state_ref[hd] = st * decay[:, sl] + kv_t
            ms = jnp.mean(o * o, axis=-1, keepdims=True)
            on = o * lax.rsqrt(ms + LN_EPS) * gn
            ga = jnp.concatenate([h_ref[_GA + 2 * hd, rows, :], h_ref[_GA + 2 * hd + 1, rows, :]],
                                 axis=1).astype(F32)
            o_ref[rows, hd * GLA_DV:(hd + 1) * GLA_DV] = (on * _silu(ga)).astype(o_ref.dtype)
            yield

    vg = [_gelu_tanh(h_ref[_VB + g].astype(F32)) for g in range(SGU_GROUPS)]
    tot = vg[0]
    for g in range(1, SGU_GROUPS):
        tot = tot + vg[g]
    mu = jnp.sum(tot, axis=-1, keepdims=True) * (1.0 / SGU_W)
    vc = [v - mu for v in vg]
    sq = vc[0] * vc[0]
    for g in range(1, SGU_GROUPS):
        sq = sq + vc[g] * vc[g]
    rstd = lax.rsqrt(jnp.sum(sq, axis=-1, keepdims=True) * (1.0 / SGU_W) + LN_EPS)
    for g in range(SGU_GROUPS):
        cols = slice(g * LANES, (g + 1) * LANES)
        vn = (vc[g] * rstd * lng_ref[:, cols] + lnb_ref[:, cols]).astype(BF16)
        ug = _gelu_tanh(h_ref[_UB + g].astype(F32))
        gate = _silu(h_ref[_GB + g].astype(F32))
        for ck in range(ts // SGU_CHUNK):
            rows = slice(ck * SGU_CHUNK, (ck + 1) * SGU_CHUNK)
            s = _dot(wt_ref[g], vn[rows]) + bias_ref[g]
            o_ref[rows, GLA_V + g * LANES:GLA_V + (g + 1) * LANES] = (
                ug[rows] * s * gate[rows]).astype(o_ref.dtype)
        yield


def _even_layer_kernel(h_ref, r_ref, wg_ref, bg_ref, gn_ref, lng_ref, lnb_ref, ws_ref, bst_ref,
                       wo_ref, x_ref, pg_ref, pb_ref, o_ref, state_ref, wt_ref, bias_ref, mix_ref, y_ref,
                       *, ts, steps, sub):
    t = pl.program_id(0)

    @pl.when(t == 0)
    def _init():
        state_ref[...] = jnp.zeros_like(state_ref)
        mix_ref[1] = jnp.zeros(mix_ref.shape[1:], mix_ref.dtype)
        r = lax.broadcasted_iota(jnp.int32, (SGU_CHUNK, SGU_CHUNK), 0)
        c = lax.broadcasted_iota(jnp.int32, (SGU_CHUNK, SGU_CHUNK), 1)
        for g in range(SGU_GROUPS):
            wt_ref[g] = jnp.where(c <= r, ws_ref[g], 0.0).astype(BF16)
            bias_ref[g] = jnp.broadcast_to(bst_ref[:, g:g + 1], (SGU_CHUNK, LANES))

    def projection(slot):
        return _out_ln_pieces(mix_ref.at[slot], wo_ref, x_ref, pg_ref, pb_ref, o_ref, y_ref, sub, OUT_TN)

    def both(slot):
        n_mix = (ts // GLA_CHUNK) * GLA_HEADS + SGU_GROUPS
        n_proj = (ts // sub) * (wo_ref.shape[1] // OUT_TN + 1)
        _interleave(_mix_pieces(h_ref, r_ref, wg_ref, bg_ref, gn_ref, lng_ref, lnb_ref, mix_ref.at[slot],
                                state_ref, wt_ref, bias_ref, ts),
                    projection(1 - slot), n_mix, n_proj)

    pl.when(jnp.logical_and(t < steps, t % 2 == 0))(functools.partial(both, 0))
    pl.when(jnp.logical_and(t < steps, t % 2 == 1))(functools.partial(both, 1))
    pl.when(t == steps)(lambda: _interleave((), projection((steps - 1) % 2)))


def _even_layer(h3, r3, wg, bg, gn, lng, lnb, ws, bst, wo, x, pg, pb, ts, sub):
    nblk, s, _ = h3.shape
    steps = s // ts
    d = wo.shape[1]
    cur = lambda t: (0, jnp.minimum(t, steps - 1), 0)
    prev = lambda t: (jnp.maximum(t - 1, 0), 0)
    const2 = lambda t: (0, 0)
    return pl.pallas_call(
        functools.partial(_even_layer_kernel, ts=ts, steps=steps, sub=sub),
        out_shape=jax.ShapeDtypeStruct((s, d), F32),
        grid=(steps + 1,),
        in_specs=[pl.BlockSpec((nblk, ts, LANES), cur),
                  pl.BlockSpec((1, ts, LANES), cur),
                  pl.BlockSpec(wg.shape, const2),
                  pl.BlockSpec(bg.shape, const2),
                  pl.BlockSpec(gn.shape, const2),
                  pl.BlockSpec(lng.shape, const2),
                  pl.BlockSpec(lnb.shape, const2),
                  pl.BlockSpec(ws.shape, lambda t: (0, 0, 0)),
                  pl.BlockSpec(bst.shape, const2),
                  pl.BlockSpec(wo.shape, const2),
                  pl.BlockSpec((ts, d), prev),
                  pl.BlockSpec(pg.shape, const2),
                  pl.BlockSpec(pb.shape, const2)],
        out_specs=pl.BlockSpec((ts, d), prev),
        scratch_shapes=[pltpu.VMEM((GLA_HEADS, GLA_DV, GLA_DK), F32),
                        pltpu.VMEM((SGU_GROUPS, SGU_CHUNK, SGU_CHUNK), BF16),
                        pltpu.VMEM((SGU_GROUPS, SGU_CHUNK, LANES), F32),
                        pltpu.VMEM((2, ts, GLA_V + SGU_W), BF16),
                        pltpu.VMEM((ts, d), F32)],
        compiler_params=pltpu.CompilerParams(
            dimension_semantics=("arbitrary",), vmem_limit_bytes=VMEM_LIMIT),
        name="even_layer",
    )(h3, r3, wg, bg, gn, lng, lnb, ws, bst, wo, x, pg, pb)


def _sb_pieces(i, q_ref, kd_ref, kp_ref, vd_ref, vp_ref, sfx_ref, acc_ref, carry_ref, blk, grp):
    nh = q_ref.shape[0]
    r = lax.broadcasted_iota(jnp.int32, (blk, blk), 0)
    c = lax.broadcasted_iota(jnp.int32, (blk, blk), 1)
    strict = c < r
    lanes = blk // LANES
    has_prev = i > 0
    for g0 in range(0, nh, grp):
        heads = range(g0, g0 + grp)
        z_d = [_dot_nt(q_ref[hd], kd_ref[hd]) for hd in heads]
        z_p = [_dot_nt(q_ref[hd], kp_ref[hd]) for hd in heads]
        yield
        a_d = [jnp.where(strict, _softplus2(z), 0.0).astype(BF16) for z in z_d]
        yield
        a_p = [_softplus2(z).astype(BF16) for z in z_p]
        yield
        cum = _dot(jnp.concatenate(a_d + a_p, axis=0), sfx_ref[...])
        for n, hd in enumerate(heads):
            c_d = cum[n * blk:(n + 1) * blk]
            c_p = cum[(grp + n) * blk:(grp + n + 1) * blk]
            tot_d = jnp.broadcast_to(c_d[:, 0:1], (blk, LANES))
            tot_p = jnp.broadcast_to(c_p[:, 0:1], (blk, LANES))
            w_d = jnp.where(strict, jnp.exp2(z_d[n] - c_d), 0.0).astype(BF16)
            w_p = jnp.exp2(z_p[n] - c_p - jnp.concatenate([tot_d] * lanes, axis=1)).astype(BF16)
            pv_d = _dot(w_d, vd_ref[hd])
            pv_p = _dot(w_p, vp_ref[hd])
            acc_ref[hd] = pv_d + jnp.where(has_prev, pv_p, 0.0)
            carry_ref[hd] = tot_d + tot_p
            yield


def _sb_deeper(i, q_ref, sfx_ref, kv_hbm, acc_ref, carry_ref, kbuf_ref, vbuf_ref, sem, blk):
    nh = q_ref.shape[0]
    lanes = blk // LANES

    def more(state):
        j, low = state
        return jnp.logical_and(j >= 0, low < F32_EXP2_UNDERFLOW)

    def copies(j, hd):
        rows = pl.ds(pl.multiple_of(j * blk, blk), blk)
        return (pltpu.make_async_copy(kv_hbm.at[SB_HEADS + hd, rows, :], kbuf_ref.at[hd], sem.at[0, hd]),
                pltpu.make_async_copy(kv_hbm.at[2 * SB_HEADS + hd, rows, :], vbuf_ref.at[hd], sem.at[1, hd]))

    def body(state):
        j, _ = state
        for hd in range(nh):
            for cp in copies(j, hd):
                cp.start()
        for hd in range(nh):
            for cp in copies(j, hd):
                cp.wait()
            z2 = _dot_nt(q_ref[hd], kbuf_ref[hd])
            cum = _dot(_softplus2(z2).astype(BF16), sfx_ref[...])
            x = z2 - cum - jnp.concatenate([carry_ref[hd]] * lanes, axis=1)
            acc_ref[hd] += _dot(jnp.exp2(x).astype(BF16), vbuf_ref[hd])
            carry_ref[hd] += jnp.broadcast_to(cum[:, 0:1], (blk, LANES))
        return j - 1, jnp.min(carry_ref[...])

    lax.while_loop(more, body, (i - 2, jnp.min(carry_ref[...])))


def _odd_layer_kernel(q_ref, kd_ref, kp_ref, vd_ref, vp_ref, g_ref, sfx_ref, kv_hbm, wo_ref, x_ref, pg_ref, pb_ref,
                      o_ref, acc_ref, carry_ref, kbuf_ref, vbuf_ref, sem, att_ref, y_ref, *, blk, grp, steps, sub):
    t = pl.program_id(0)

    @pl.when(t == 0)
    def _init():
        att_ref[1] = jnp.zeros(att_ref.shape[1:], att_ref.dtype)

    def projection(slot):
        return _out_ln_pieces(att_ref.at[slot], wo_ref, x_ref, pg_ref, pb_ref, o_ref, y_ref, sub, OUT_TN)

    def both(slot):
        n_att = (q_ref.shape[0] // grp) * (3 + grp)
        n_proj = (blk // sub) * (wo_ref.shape[1] // OUT_TN + 1)
        _interleave(_sb_pieces(t, q_ref, kd_ref, kp_ref, vd_ref, vp_ref, sfx_ref, acc_ref, carry_ref, blk, grp),
                    projection(1 - slot), n_att, n_proj)
        _sb_deeper(t, q_ref, sfx_ref, kv_hbm, acc_ref, carry_ref, kbuf_ref, vbuf_ref, sem, blk)
        for hd in range(q_ref.shape[0]):
            att_ref[slot, :, hd * SB_DIM:(hd + 1) * SB_DIM] = (
                acc_ref[hd] * _silu(g_ref[hd].astype(F32))).astype(att_ref.dtype)

    pl.when(jnp.logical_and(t < steps, t % 2 == 0))(functools.partial(both, 0))
    pl.when(jnp.logical_and(t < steps, t % 2 == 1))(functools.partial(both, 1))
    pl.when(t == steps)(lambda: _interleave((), projection((steps - 1) % 2)))


def _odd_layer(h3, wo, x, pg, pb, blk, grp, sub):
    _, s, _ = h3.shape
    nh = SB_HEADS
    d = wo.shape[1]
    steps = s // blk
    rs = lax.broadcasted_iota(jnp.int32, (blk, blk), 0)
    cs = lax.broadcasted_iota(jnp.int32, (blk, blk), 1)
    sfx = jnp.where(rs >= cs, 1.0, 0.0).astype(BF16)
    here = lambda part: (lambda t: (part, jnp.minimum(t, steps - 1), 0))
    before = lambda part: (lambda t: (part, jnp.maximum(jnp.minimum(t, steps - 1) - 1, 0), 0))
    prev = lambda t: (jnp.maximum(t - 1, 0), 0)
    const2 = lambda t: (0, 0)
    head_block = (nh, blk, SB_DIM)
    return pl.pallas_call(
        functools.partial(_odd_layer_kernel, blk=blk, grp=grp, steps=steps, sub=sub),
        out_shape=jax.ShapeDtypeStruct((s, d), F32),
        grid=(steps + 1,),
        in_specs=[pl.BlockSpec(head_block, here(0)),
                  pl.BlockSpec(head_block, here(1)),
                  pl.BlockSpec(head_block, before(1)),
                  pl.BlockSpec(head_block, here(2)),
                  pl.BlockSpec(head_block, before(2)),
                  pl.BlockSpec(head_block, here(3)),
                  pl.BlockSpec(sfx.shape, const2),
                  pl.BlockSpec(memory_space=pl.ANY),
                  pl.BlockSpec(wo.shape, const2),
                  pl.BlockSpec((blk, d), prev),
                  pl.BlockSpec(pg.shape, const2),
                  pl.BlockSpec(pb.shape, const2)],
        out_specs=pl.BlockSpec((blk, d), prev),
        scratch_shapes=[pltpu.VMEM((nh, blk, SB_DIM), F32), pltpu.VMEM((nh, blk, LANES), F32),
                        pltpu.VMEM(head_block, BF16), pltpu.VMEM(head_block, BF16),
                        pltpu.SemaphoreType.DMA((2, nh)),
                        pltpu.VMEM((2, blk, nh * SB_DIM), BF16), pltpu.VMEM((blk, d), F32)],
        compiler_params=pltpu.CompilerParams(
            dimension_semantics=("arbitrary",), vmem_limit_bytes=VMEM_LIMIT),
        name="odd_layer",
    )(h3, h3, h3, h3, h3, h3, sfx, h3, wo, x, pg, pb)


def _pick(n, candidates):
    for c in candidates:
        if n % c == 0:
            return c
    raise ValueError(f"no tile of {candidates} divides {n}")


def kernel(x, even_w_in, even_gla_w_gate2, even_gla_b_gate, even_gla_norm_g, even_sgu_ln_g, even_sgu_ln_b,
           even_sgu_w_s, even_sgu_b_s, even_w_out, odd_w_in, odd_w_out, post_ln_g, post_ln_b):
    bsz, seq, d = x.shape
    assert bsz == 1 and d == D_MODEL
    xs = x[0]
    tm = _pick(seq, (1024, 512, 256))
    tl = _pick(seq, (256,))

    wt = jnp.swapaxes(even_w_in[0], 0, 1)
    o_r = 2 * GLA_QK + GLA_V
    assert o_r % EVEN_TN == 0
    w_rank = jnp.concatenate([wt[o_r:o_r + GLA_RANK], jnp.zeros((LANES - GLA_RANK, d), F32)], axis=0).astype(BF16)
    wg = jnp.concatenate([even_gla_w_gate2[0], jnp.zeros((LANES - GLA_RANK, GLA_QK), F32)], axis=0).astype(BF16)
    h3, r3 = _project(xs, wt, jnp.ones((1, EVEN_BLOCKS * LANES), F32), tm, EVEN_TN, True,
                      row_start=lambda j: j * EVEN_TN + jnp.where(j * EVEN_TN >= o_r, GLA_RANK, 0),
                      side_w=w_rank)
    x1 = _even_layer(h3, r3, wg, even_gla_b_gate[0][None, :], even_gla_norm_g[0][None, :],
                     even_sgu_ln_g[0][None, :], even_sgu_ln_b[0][None, :],
                     even_sgu_w_s[0], even_sgu_b_s[0].T, even_w_out[0].astype(BF16), xs,
                     post_ln_g[0][None, :], post_ln_b[0][None, :], tl, EVEN_OUT_SUB)

    q_scale = jnp.concatenate([jnp.full((1, SB_HEADS * SB_DIM), SB_DIM ** -0.5 * LOG2E, F32),
                               jnp.ones((1, 3 * SB_HEADS * SB_DIM), F32)], axis=1)
    h3, = _project(x1, odd_w_in[0], q_scale, tm, 1024, False)
    x2 = _odd_layer(h3, odd_w_out[0].astype(BF16), x1, post_ln_g[1][None, :], post_ln_b[1][None, :],
                    SB_BLOCK, SB_GROUP, ODD_OUT_SUB)
    return x2[None]
```

```python
import functools

import jax
import jax.numpy as jnp
from jax import lax
from jax.experimental import pallas as pl
from jax.experimental.pallas import tpu as pltpu

F32 = jnp.float32
BF16 = jnp.bfloat16

LANES = 128
BF16_SUBLANES = 16
D_MODEL = 2048
DEPTH = 2
GLA_HEADS = 4
GLA_DK = 128
GLA_DV = 256
GLA_QK = GLA_HEADS * GLA_DK
GLA_V = GLA_HEADS * GLA_DV
GLA_RANK = 16
GLA_TAU = 16.0
GLA_CHUNK = 64
SGU_W = 1024
SGU_CHUNK = 128
SGU_GROUPS = SGU_W // LANES
SB_HEADS = 16
SB_DIM = 128
ALPHA = (2.0 * DEPTH) ** 0.25
LN_EPS = 1e-5

_QA, _KA, _VA, _GA, _UB, _VB, _GB = 0, 4, 8, 16, 24, 32, 40
EVEN_BLOCKS = 48
EVEN_TN = 1024
SB_BLOCK = 256
SB_GROUP = 4
EVEN_OUT_TN = 512
ODD_OUT_TN = 256
EVEN_OUT_SUB = 128
ODD_OUT_SUB = 256

VMEM_LIMIT = 56 * 1024 * 1024
LOG2E = 1.4426950408889634
F32_EXP2_UNDERFLOW = 151.0


def _softplus(x):
    return jnp.maximum(x, 0.0) + jnp.log(1.0 + jnp.exp(-jnp.abs(x)))


def _softplus2(z2):
    return jnp.maximum(z2, 0.0) + jnp.log(1.0 + jnp.exp2(-jnp.abs(z2))) * LOG2E


def _silu(x):
    return x / (1.0 + jnp.exp2(x * -LOG2E))


def _gelu_tanh(x):
    k1 = -2.0 * 0.7978845608028654 * LOG2E
    return x / (1.0 + jnp.exp2(x * (k1 + (k1 * 0.044715) * (x * x))))


def _split_bf16(x):
    hi = x.astype(BF16)
    lo = (x - hi.astype(F32)).astype(BF16)
    return hi, lo


def _dot(a, b):
    return jnp.dot(a, b, preferred_element_type=F32)


def _dot_nt(a, b):
    return lax.dot_general(a, b, (((1,), (1,)), ((), ())), preferred_element_type=F32)


def _proj_kernel(*refs, w_transposed, has_side):
    if has_side:
        a_ref, w_ref, cs_ref, ws_ref, o_ref, os_ref, ab_ref = refs
    else:
        a_ref, w_ref, cs_ref, o_ref, ab_ref = refs

    @pl.when(pl.program_id(1) == 0)
    def _first():
        ab_ref[...] = a_ref[...].astype(BF16)
        if has_side:
            os_ref[0] = _dot_nt(ab_ref[...], ws_ref[...]).astype(os_ref.dtype)

    dot = _dot_nt if w_transposed else _dot
    r = dot(ab_ref[...], w_ref[...]) * cs_ref[...]
    for c in range(o_ref.shape[0]):
        o_ref[c] = r[:, c * LANES:(c + 1) * LANES].astype(o_ref.dtype)


def _project(a, w, col_scale, tm, tn, w_transposed, row_start=None, side_w=None):
    m, k = a.shape
    n = col_scale.shape[1]
    nb = tn // LANES
    if not w_transposed:
        w_spec = pl.BlockSpec((k, tn), lambda i, j: (0, j))
    elif row_start is None:
        w_spec = pl.BlockSpec((tn, k), lambda i, j: (j, 0))
    else:
        w_spec = pl.BlockSpec((pl.Element(tn), pl.Element(k)),
                              lambda i, j: (pl.multiple_of(row_start(j), BF16_SUBLANES), 0))
    in_specs = [pl.BlockSpec((tm, k), lambda i, j: (i, 0)), w_spec, pl.BlockSpec((1, tn), lambda i, j: (0, j))]
    out_shape = [jax.ShapeDtypeStruct((n // LANES, m, LANES), BF16)]
    out_specs = [pl.BlockSpec((nb, tm, LANES), lambda i, j: (j, i, 0))]
    operands = [a, w, col_scale]
    if side_w is not None:
        in_specs.append(pl.BlockSpec(side_w.shape, lambda i, j: (0, 0)))
        out_shape.append(jax.ShapeDtypeStruct((1, m, LANES), BF16))
        out_specs.append(pl.BlockSpec((1, tm, LANES), lambda i, j: (0, i, 0)))
        operands.append(side_w)
    return pl.pallas_call(
        functools.partial(_proj_kernel, w_transposed=w_transposed, has_side=side_w is not None),
        out_shape=out_shape,
        grid=(m // tm, n // tn),
        in_specs=in_specs,
        out_specs=out_specs,
        scratch_shapes=[pltpu.VMEM((tm, k), BF16)],
        compiler_params=pltpu.CompilerParams(
            dimension_semantics=("arbitrary", "arbitrary"), vmem_limit_bytes=VMEM_LIMIT),
        name="in_proj",
    )(*operands)


def _out_ln_pieces(a_ref, w_ref, x_ref, g_ref, b_ref, o_ref, y_ref, sub, tn):
    n = w_ref.shape[1]
    for t in range(o_ref.shape[0] // sub):
        rows = slice(t * sub, (t + 1) * sub)
        for c in range(n // tn):
            cols = slice(c * tn, (c + 1) * tn)
            y_ref[rows, cols] = _dot(a_ref[rows, :], w_ref[:, cols])
            yield
        r = ALPHA * x_ref[rows, :] + y_ref[rows, :]
        mu = jnp.mean(r, axis=-1, keepdims=True)
        rc = r - mu
        var = jnp.mean(rc * rc, axis=-1, keepdims=True)
        o_ref[rows, :] = rc * lax.rsqrt(var + LN_EPS) * g_ref[...] + b_ref[...]
        yield


def _interleave(main, side, n_main=1, n_side=1):
    for k, _ in enumerate(main):
        for _ in range((k + 1) * n_side // n_main - k * n_side // n_main):
            next(side, None)
    for _ in side:
        pass


def _mix_pieces(h_ref, r_ref, wg_ref, bg_ref, gn_ref, lng_ref, lnb_ref, o_ref, state_ref, wt_ref, bias_ref, ts):
    u = _dot(r_ref[0], wg_ref[...]) + bg_ref[...]
    log_a = -_softplus(-u) * (1.0 / GLA_TAU)
    r = lax.broadcasted_iota(jnp.int32, (ts, ts), 0)
    c = lax.broadcasted_iota(jnp.int32, (ts, ts), 1)
    same_chunk = (r // GLA_CHUNK) == (c // GLA_CHUNK)
    cum_mat = jnp.where(same_chunk & (c <= r), 1.0, 0.0).astype(BF16)
    la_hi, la_lo = _split_bf16(log_a)
    b_all = _dot(cum_mat, la_hi) + _dot(cum_mat, la_lo)

    rr = lax.broadcasted_iota(jnp.int32, (GLA_CHUNK, GLA_CHUNK), 0)
    cc = lax.broadcasted_iota(jnp.int32, (GLA_CHUNK, GLA_CHUNK), 1)
    causal = cc <= rr
    scale = GLA_DK ** -0.5
    gn = gn_ref[...]

    for ck in range(ts // GLA_CHUNK):
        rows = slice(ck * GLA_CHUNK, (ck + 1) * GLA_CHUNK)
        bc = b_all[rows]
        b_mid = bc[GLA_CHUNK // 2:GLA_CHUNK // 2 + 1]
        b_last = bc[GLA_CHUNK - 1:GLA_CHUNK]
        e_q = jnp.exp(bc - b_mid) * scale
        e_k = jnp.exp(b_mid - bc)
        e_kv = jnp.exp(b_last - bc)
        e_b = jnp.exp(bc) * scale
        decay = jnp.exp(b_last)
        for hd in range(GLA_HEADS):
            sl = slice(hd * GLA_DK, (hd + 1) * GLA_DK)
            qh = h_ref[_QA + hd, rows, :].astype(F32)
            kh = h_ref[_KA + hd, rows, :].astype(F32)
            vh = jnp.concatenate([h_ref[_VA + 2 * hd, rows, :], h_ref[_VA + 2 * hd + 1, rows, :]], axis=1)
            sc = _dot_nt((qh * e_q[:, sl]).astype(BF16), (kh * e_k[:, sl]).astype(BF16))
            sc = jnp.where(causal, sc, 0.0).astype(BF16)
            o = _dot(sc, vh)
            st = state_ref[hd]
            o = o + _dot_nt((qh * e_b[:, sl]).astype(BF16), st.astype(BF16))
            kk = (kh * e_kv[:, sl]).astype(BF16)
            kv_t = lax.dot_general(vh, kk, (((0,), (0,)), ((), ())), preferred_element_type=F32)
            state_ref[hd] = st * decay[:, sl] + kv_t
            ms = jnp.mean(o * o, axis=-1, keepdims=True)
            on = o * lax.rsqrt(ms + LN_EPS) * gn
            ga = jnp.concatenate([h_ref[_GA + 2 * hd, rows, :], h_ref[_GA + 2 * hd + 1, rows, :]],
                                 axis=1).astype(F32)
            o_ref[rows, hd * GLA_DV:(hd + 1) * GLA_DV] = (on * _silu(ga)).astype(o_ref.dtype)
            yield

    vg = [_gelu_tanh(h_ref[_VB + g].astype(F32)) for g in range(SGU_GROUPS)]
    tot = vg[0]
    for g in range(1, SGU_GROUPS):
        tot = tot + vg[g]
    mu = jnp.sum(tot, axis=-1, keepdims=True) * (1.0 / SGU_W)
    vc = [v - mu for v in vg]
    sq = vc[0] * vc[0]
    for g in range(1, SGU_GROUPS):
        sq = sq + vc[g] * vc[g]
    rstd = lax.rsqrt(jnp.sum(sq, axis=-1, keepdims=True) * (1.0 / SGU_W) + LN_EPS)
    for g in range(SGU_GROUPS):
        cols = slice(g * LANES, (g + 1) * LANES)
        vn = (vc[g] * rstd * lng_ref[:, cols] + lnb_ref[:, cols]).astype(BF16)
        ug = _gelu_tanh(h_ref[_UB + g].astype(F32))
        gate = _silu(h_ref[_GB + g].astype(F32))
        for ck in range(ts // SGU_CHUNK):
            rows = slice(ck * SGU_CHUNK, (ck + 1) * SGU_CHUNK)
            s = _dot(wt_ref[g], vn[rows]) + bias_ref[g]
            o_ref[rows, GLA_V + g * LANES:GLA_V + (g + 1) * LANES] = (
                ug[rows] * s * gate[rows]).astype(o_ref.dtype)
        yield


def _even_layer_kernel(h_ref, r_ref, wg_ref, bg_ref, gn_ref, lng_ref, lnb_ref, ws_ref, bst_ref,
                       wo_ref, x_ref, pg_ref, pb_ref, o_ref, state_ref, wt_ref, bias_ref, mix_ref, y_ref,
                       *, ts, steps, sub):
    t = pl.program_id(0)

    @pl.when(t == 0)
    def _init():
        state_ref[...] = jnp.zeros_like(state_ref)
        mix_ref[1] = jnp.zeros(mix_ref.shape[1:], mix_ref.dtype)
        r = lax.broadcasted_iota(jnp.int32, (SGU_CHUNK, SGU_CHUNK), 0)
        c = lax.broadcasted_iota(jnp.int32, (SGU_CHUNK, SGU_CHUNK), 1)
        for g in range(SGU_GROUPS):
            wt_ref[g] = jnp.where(c <= r, ws_ref[g], 0.0).astype(BF16)
            bias_ref[g] = jnp.broadcast_to(bst_ref[:, g:g + 1], (SGU_CHUNK, LANES))

    def projection(slot):
        return _out_ln_pieces(mix_ref.at[slot], wo_ref, x_ref, pg_ref, pb_ref, o_ref, y_ref, sub, EVEN_OUT_TN)

    def both(slot):
        n_mix = (ts // GLA_CHUNK) * GLA_HEADS + SGU_GROUPS
        n_proj = (ts // sub) * (wo_ref.shape[1] // EVEN_OUT_TN + 1)
        _interleave(_mix_pieces(h_ref, r_ref, wg_ref, bg_ref, gn_ref, lng_ref, lnb_ref, mix_ref.at[slot],
                                state_ref, wt_ref, bias_ref, ts),
                    projection(1 - slot), n_mix, n_proj)

    pl.when(jnp.logical_and(t < steps, t % 2 == 0))(functools.partial(both, 0))
    pl.when(jnp.logical_and(t < steps, t % 2 == 1))(functools.partial(both, 1))
    pl.when(t == steps)(lambda: _interleave((), projection((steps - 1) % 2)))


def _even_layer(h3, r3, wg, bg, gn, lng, lnb, ws, bst, wo, x, pg, pb, ts, sub):
    nblk, s, _ = h3.shape
    steps = s // ts
    d = wo.shape[1]
    cur = lambda t: (0, jnp.minimum(t, steps - 1), 0)
    prev = lambda t: (jnp.maximum(t - 1, 0), 0)
    const2 = lambda t: (0, 0)
    return pl.pallas_call(
        functools.partial(_even_layer_kernel, ts=ts, steps=steps, sub=sub),
        out_shape=jax.ShapeDtypeStruct((s, d), F32),
        grid=(steps + 1,),
        in_specs=[pl.BlockSpec((nblk, ts, LANES), cur),
                  pl.BlockSpec((1, ts, LANES), cur),
                  pl.BlockSpec(wg.shape, const2),
                  pl.BlockSpec(bg.shape, const2),
                  pl.BlockSpec(gn.shape, const2),
                  pl.BlockSpec(lng.shape, const2),
                  pl.BlockSpec(lnb.shape, const2),
                  pl.BlockSpec(ws.shape, lambda t: (0, 0, 0)),
                  pl.BlockSpec(bst.shape, const2),
                  pl.BlockSpec(wo.shape, const2),
                  pl.BlockSpec((ts, d), prev),
                  pl.BlockSpec(pg.shape, const2),
                  pl.BlockSpec(pb.shape, const2)],
        out_specs=pl.BlockSpec((ts, d), prev),
        scratch_shapes=[pltpu.VMEM((GLA_HEADS, GLA_DV, GLA_DK), F32),
                        pltpu.VMEM((SGU_GROUPS, SGU_CHUNK, SGU_CHUNK), BF16),
                        pltpu.VMEM((SGU_GROUPS, SGU_CHUNK, LANES), F32),
                        pltpu.VMEM((2, ts, GLA_V + SGU_W), BF16),
                        pltpu.VMEM((ts, d), F32)],
        compiler_params=pltpu.CompilerParams(
            dimension_semantics=("arbitrary",), vmem_limit_bytes=VMEM_LIMIT),
        name="even_layer",
    )(h3, r3, wg, bg, gn, lng, lnb, ws, bst, wo, x, pg, pb)


def _sb_pieces(i, q_ref, kd_ref, kp_ref, vd_ref, vp_ref, sfx_ref, acc_ref, carry_ref, blk, grp):
    nh = q_ref.shape[0]
    r = lax.broadcasted_iota(jnp.int32, (blk, blk), 0)
    c = lax.broadcasted_iota(jnp.int32, (blk, blk), 1)
    strict = c < r
    lanes = blk // LANES
    has_prev = i > 0
    for g0 in range(0, nh, grp):
        heads = range(g0, g0 + grp)
        z_d = [_dot_nt(q_ref[hd], kd_ref[hd]) for hd in heads]
        z_p = [_dot_nt(q_ref[hd], kp_ref[hd]) for hd in heads]
        yield
        a_d = [jnp.where(strict, _softplus2(z), 0.0).astype(BF16) for z in z_d]
        yield
        a_p = [_softplus2(z).astype(BF16) for z in z_p]
        yield
        cum = _dot(jnp.concatenate(a_d + a_p, axis=0), sfx_ref[...])
        for n, hd in enumerate(heads):
            c_d = cum[n * blk:(n + 1) * blk]
            c_p = cum[(grp + n) * blk:(grp + n + 1) * blk]
            tot_d = jnp.broadcast_to(c_d[:, 0:1], (blk, LANES))
            tot_p = jnp.broadcast_to(c_p[:, 0:1], (blk, LANES))
            w_d = jnp.where(strict, jnp.exp2(z_d[n] - c_d), 0.0).astype(BF16)
            w_p = jnp.exp2(z_p[n] - c_p - jnp.concatenate([tot_d] * lanes, axis=1)).astype(BF16)
            pv_d = _dot(w_d, vd_ref[hd])
            pv_p = _dot(w_p, vp_ref[hd])
            acc_ref[hd] = pv_d + jnp.where(has_prev, pv_p, 0.0)
            carry_ref[hd] = tot_d + tot_p
            yield


def _sb_deeper(i, q_ref, sfx_ref, kv_hbm, acc_ref, carry_ref, kbuf_ref, vbuf_ref, sem, blk):
    nh = q_ref.shape[0]
    lanes = blk // LANES

    def more(state):
        j, low = state
        return jnp.logical_and(j >= 0, low < F32_EXP2_UNDERFLOW)

    def copies(j, hd):
        rows = pl.ds(pl.multiple_of(j * blk, blk), blk)
        return (pltpu.make_async_copy(kv_hbm.at[SB_HEADS + hd, rows, :], kbuf_ref.at[hd], sem.at[0, hd]),
                pltpu.make_async_copy(kv_hbm.at[2 * SB_HEADS + hd, rows, :], vbuf_ref.at[hd], sem.at[1, hd]))

    def body(state):
        j, _ = state
        for hd in range(nh):
            for cp in copies(j, hd):
                cp.start()
        for hd in range(nh):
            for cp in copies(j, hd):
                cp.wait()
            z2 = _dot_nt(q_ref[hd], kbuf_ref[hd])
            cum = _dot(_softplus2(z2).astype(BF16), sfx_ref[...])
            x = z2 - cum - jnp.concatenate([carry_ref[hd]] * lanes, axis=1)
            acc_ref[hd] += _dot(jnp.exp2(x).astype(BF16), vbuf_ref[hd])
            carry_ref[hd] += jnp.broadcast_to(cum[:, 0:1], (blk, LANES))
        return j - 1, jnp.min(carry_ref[...])

    lax.while_loop(more, body, (i - 2, jnp.min(carry_ref[...])))


def _odd_layer_kernel(q_ref, kd_ref, kp_ref, vd_ref, vp_ref, g_ref, sfx_ref, kv_hbm, wo_ref, x_ref, pg_ref, pb_ref,
                      o_ref, acc_ref, carry_ref, kbuf_ref, vbuf_ref, sem, att_ref, y_ref, *, blk, grp, steps, sub):
    t = pl.program_id(0)

    @pl.when(t == 0)
    def _init():
        att_ref[1] = jnp.zeros(att_ref.shape[1:], att_ref.dtype)

    def projection(slot):
        return _out_ln_pieces(att_ref.at[slot], wo_ref, x_ref, pg_ref, pb_ref, o_ref, y_ref, sub, ODD_OUT_TN)

    def both(slot):
        n_att = (q_ref.shape[0] // grp) * (3 + grp)
        n_proj = (blk // sub) * (wo_ref.shape[1] // ODD_OUT_TN + 1)
        _interleave(_sb_pieces(t, q_ref, kd_ref, kp_ref, vd_ref, vp_ref, sfx_ref, acc_ref, carry_ref, blk, grp),
                    projection(1 - slot), n_att, n_proj)
        _sb_deeper(t, q_ref, sfx_ref, kv_hbm, acc_ref, carry_ref, kbuf_ref, vbuf_ref, sem, blk)
        for hd in range(q_ref.shape[0]):
            att_ref[slot, :, hd * SB_DIM:(hd + 1) * SB_DIM] = (
                acc_ref[hd] * _silu(g_ref[hd].astype(F32))).astype(att_ref.dtype)

    pl.when(jnp.logical_and(t < steps, t % 2 == 0))(functools.partial(both, 0))
    pl.when(jnp.logical_and(t < steps, t % 2 == 1))(functools.partial(both, 1))
    pl.when(t == steps)(lambda: _interleave((), projection((steps - 1) % 2)))


def _odd_layer(h3, wo, x, pg, pb, blk, grp, sub):
    _, s, _ = h3.shape
    nh = SB_HEADS
    d = wo.shape[1]
    steps = s // blk
    rs = lax.broadcasted_iota(jnp.int32, (blk, blk), 0)
    cs = lax.broadcasted_iota(jnp.int32, (blk, blk), 1)
    sfx = jnp.where(rs >= cs, 1.0, 0.0).astype(BF16)
    here = lambda part: (lambda t: (part, jnp.minimum(t, steps - 1), 0))
    before = lambda part: (lambda t: (part, jnp.maximum(jnp.minimum(t, steps - 1) - 1, 0), 0))
    prev = lambda t: (jnp.maximum(t - 1, 0), 0)
    const2 = lambda t: (0, 0)
    head_block = (nh, blk, SB_DIM)
    return pl.pallas_call(
        functools.partial(_odd_layer_kernel, blk=blk, grp=grp, steps=steps, sub=sub),
        out_shape=jax.ShapeDtypeStruct((s, d), F32),
        grid=(steps + 1,),
        in_specs=[pl.BlockSpec(head_block, here(0)),
                  pl.BlockSpec(head_block, here(1)),
                  pl.BlockSpec(head_block, before(1)),
                  pl.BlockSpec(head_block, here(2)),
                  pl.BlockSpec(head_block, before(2)),
                  pl.BlockSpec(head_block, here(3)),
                  pl.BlockSpec(sfx.shape, const2),
                  pl.BlockSpec(memory_space=pl.ANY),
                  pl.BlockSpec(wo.shape, const2),
                  pl.BlockSpec((blk, d), prev),
                  pl.BlockSpec(pg.shape, const2),
                  pl.BlockSpec(pb.shape, const2)],
        out_specs=pl.BlockSpec((blk, d), prev),
        scratch_shapes=[pltpu.VMEM((nh, blk, SB_DIM), F32), pltpu.VMEM((nh, blk, LANES), F32),
                        pltpu.VMEM(head_block, BF16), pltpu.VMEM(head_block, BF16),
                        pltpu.SemaphoreType.DMA((2, nh)),
                        pltpu.VMEM((2, blk, nh * SB_DIM), BF16), pltpu.VMEM((blk, d), F32)],
        compiler_params=pltpu.CompilerParams(
            dimension_semantics=("arbitrary",), vmem_limit_bytes=VMEM_LIMIT),
        name="odd_layer",
    )(h3, h3, h3, h3, h3, h3, sfx, h3, wo, x, pg, pb)


def _pick(n, candidates):
    for c in candidates:
        if n % c == 0:
            return c
    raise ValueError(f"no tile of {candidates} divides {n}")


def kernel(x, even_w_in, even_gla_w_gate2, even_gla_b_gate, even_gla_norm_g, even_sgu_ln_g, even_sgu_ln_b,
           even_sgu_w_s, even_sgu_b_s, even_w_out, odd_w_in, odd_w_out, post_ln_g, post_ln_b):
    bsz, seq, d = x.shape
    assert bsz == 1 and d == D_MODEL
    xs = x[0]
    tm = _pick(seq, (1024, 512, 256))
    tl = _pick(seq, (256,))

    wt = jnp.swapaxes(even_w_in[0], 0, 1).astype(BF16)
    o_r = 2 * GLA_QK + GLA_V
    assert o_r % EVEN_TN == 0
    w_rank = jnp.concatenate([wt[o_r:o_r + GLA_RANK], jnp.zeros((LANES - GLA_RANK, d), BF16)], axis=0)
    wg = jnp.concatenate([even_gla_w_gate2[0], jnp.zeros((LANES - GLA_RANK, GLA_QK), F32)], axis=0).astype(BF16)
    h3, r3 = _project(xs, wt, jnp.ones((1, EVEN_BLOCKS * LANES), F32), tm, EVEN_TN, True,
                      row_start=lambda j: j * EVEN_TN + jnp.where(j * EVEN_TN >= o_r, GLA_RANK, 0),
                      side_w=w_rank)
    x1 = _even_layer(h3, r3, wg, even_gla_b_gate[0][None, :], even_gla_norm_g[0][None, :],
                     even_sgu_ln_g[0][None, :], even_sgu_ln_b[0][None, :],
                     even_sgu_w_s[0], even_sgu_b_s[0].T, even_w_out[0].astype(BF16), xs,
                     post_ln_g[0][None, :], post_ln_b[0][None, :], tl, EVEN_OUT_SUB)

    q_scale = jnp.concatenate([jnp.full((1, SB_HEADS * SB_DIM), SB_DIM ** -0.5 * LOG2E, F32),
                               jnp.ones((1, 3 * SB_HEADS * SB_DIM), F32)], axis=1)
    h3, = _project(x1, odd_w_in[0].astype(BF16), q_scale, tm, 1024, False)
    x2 = _odd_layer(h3, odd_w_out[0].astype(BF16), x1, post_ln_g[1][None, :], post_ln_b[1][None, :],
                    SB_BLOCK, SB_GROUP, ODD_OUT_SUB)
    return x2[None]
```
